```python
import math
import jax, jax.numpy as jnp
from jax import lax
import numpy as np

D_MODEL = 1024
BATCH = 8
SEQ = 4096
DEPTH = 1

D_INNER = 2 * D_MODEL
SSM_HEADDIM = 64
SSM_HEADS = D_INNER // SSM_HEADDIM
SSM_GROUPS = 8
SSM_HEADS_PER_GROUP = SSM_HEADS // SSM_GROUPS
SSM_STATE = 128
SSM_CONV = 4
SSM_CHUNK = 128
CONV_DIM = D_INNER + 2 * SSM_GROUPS * SSM_STATE
CONF_DIM = D_MODEL
CONF_KERNEL = 31
N_EXPERTS = 32
TOP_K = 4
D_FF = D_MODEL
SWIGLU_LIMIT = 7.0
SWIGLU_ALPHA = 1.702
MOE_BLOCK = 128
IN_COLS = D_INNER + CONV_DIM + SSM_HEADS + 2 * CONF_DIM + 2 * D_MODEL
EPS = 1e-6

kernel_name = 'hybrid_ssd_conformer_moe_block'


def rms_norm(x, w):
    xf = x.astype(jnp.float32)
    y = xf * lax.rsqrt(jnp.mean(xf * xf, axis=-1, keepdims=True) + EPS)
    return (y * w.astype(jnp.float32)).astype(x.dtype)


def group_rms_norm(y, w, groups):
    b, s, d = y.shape
    yf = y.astype(jnp.float32).reshape(b, s, groups, d // groups)
    yf = yf * lax.rsqrt(jnp.mean(yf * yf, axis=-1, keepdims=True) + EPS)
    return (yf.reshape(b, s, d) * w.astype(jnp.float32)).astype(y.dtype)


def layer_norm(x, w, b):
    xf = x.astype(jnp.float32)
    mu = jnp.mean(xf, axis=-1, keepdims=True)
    var = jnp.mean(jnp.square(xf - mu), axis=-1, keepdims=True)
    y = (xf - mu) * lax.rsqrt(var + EPS)
    return (y * w.astype(jnp.float32) + b.astype(jnp.float32)).astype(x.dtype)


def modulate(h, shift, scale):
    return h * (1.0 + scale[:, None, :]) + shift[:, None, :]


def causal_depthwise_conv(u, w, b):
    width, ch = w.shape
    out = lax.conv_general_dilated(
        u, w[:, None, :].astype(u.dtype), window_strides=(1,), padding=[(width - 1, 0)],
        dimension_numbers=('NWC', 'WIO', 'NWC'), feature_group_count=ch)
    return out + b


def ssd_chunked(xh, dt, a, bm, cm):
    b, s, _, _ = xh.shape
    nc, L, G, R = s // SSM_CHUNK, SSM_CHUNK, SSM_GROUPS, SSM_HEADS_PER_GROUP
    xf = (xh.astype(jnp.float32) * dt[..., None]).reshape(b, nc, L, G, R, SSM_HEADDIM)
    a_dt = (dt * a).reshape(b, nc, L, G, R)
    a_cs = jnp.cumsum(a_dt, axis=2)
    bc = bm.astype(jnp.float32).reshape(b, nc, L, G, SSM_STATE)
    cc = cm.astype(jnp.float32).reshape(b, nc, L, G, SSM_STATE)
    causal = jnp.tril(jnp.ones((L, L), dtype=bool))[None, None, :, :, None, None]
    seg = a_cs[:, :, :, None] - a_cs[:, :, None]
    decay_ls = jnp.exp(jnp.where(causal, seg, -jnp.inf))
    cb = jnp.einsum('bclgn,bcsgn->bclsg', cc, bc)
    y_diag = jnp.einsum('bclsg,bclsgr,bcsgrp->bclgrp', cb, decay_ls, xf)
    decay_to_end = jnp.exp(a_cs[:, :, -1:] - a_cs)
    chunk_states = jnp.einsum('bcsgn,bcsgr,bcsgrp->bcgrpn', bc, decay_to_end, xf)
    chunk_decay = jnp.exp(a_cs[:, :, -1])

    def step(state, inp):
        dec, st = inp
        return state * dec[..., None, None] + st, state

    init = jnp.zeros((b, G, R, SSM_HEADDIM, SSM_STATE), jnp.float32)
    _, states_in = lax.scan(step, init, (jnp.moveaxis(chunk_decay, 1, 0), jnp.moveaxis(chunk_states, 1, 0)))
    states_in = jnp.moveaxis(states_in, 0, 1)
    y_off = jnp.einsum('bclgn,bclgr,bcgrpn->bclgrp', cc, jnp.exp(a_cs), states_in)
    return (y_diag + y_off).reshape(b, s, SSM_HEADS, SSM_HEADDIM).astype(xh.dtype)


def hybrid_mixer(h, w_in, conv_ssm_w, conv_ssm_b, dt_bias, a_log, d_skip, ssm_norm_w, w_ssm_out,
                 conf_dw_w, conf_dw_b, conf_ln_w, conf_ln_b, w_conf_out, w_o):
    b, s, _ = h.shape
    proj = h @ w_in
    cuts = np.cumsum([D_INNER, CONV_DIM, SSM_HEADS, 2 * CONF_DIM]).tolist()
    z, xbc, dt_raw, glu_in, gate_logits = jnp.split(proj, cuts, axis=-1)
    xbc = jax.nn.silu(causal_depthwise_conv(xbc, conv_ssm_w, conv_ssm_b))
    xs, bm, cm = jnp.split(xbc, [D_INNER, D_INNER + SSM_GROUPS * SSM_STATE], axis=-1)
    xh = xs.reshape(b, s, SSM_HEADS, SSM_HEADDIM)
    dt = jax.nn.softplus(dt_raw.astype(jnp.float32) + dt_bias.astype(jnp.float32))
    a = -jnp.exp(a_log.astype(jnp.float32))
    y = ssd_chunked(xh, dt, a, bm.reshape(b, s, SSM_GROUPS, SSM_STATE), cm.reshape(b, s, SSM_GROUPS, SSM_STATE))
    y = (y + d_skip[:, None] * xh).reshape(b, s, D_INNER)
    y = group_rms_norm(y * jax.nn.silu(z), ssm_norm_w, SSM_GROUPS)
    y_a = y @ w_ssm_out
    u_val, u_gate = jnp.split(glu_in, 2, axis=-1)
    u = u_val * jax.nn.sigmoid(u_gate)
    u = causal_depthwise_conv(u, conf_dw_w, conf_dw_b)
    u = jax.nn.silu(layer_norm(u, conf_ln_w, conf_ln_b))
    y_b = u @ w_conf_out
    g_a, g_b = jnp.split(gate_logits, 2, axis=-1)
    merged = jax.nn.sigmoid(g_a) * y_a + jax.nn.sigmoid(g_b) * y_b
    return merged @ w_o


def clamped_swiglu(gu):
    x_glu, x_lin = gu[..., ::2], gu[..., 1::2]
    x_glu = jnp.minimum(x_glu, SWIGLU_LIMIT)
    x_lin = jnp.clip(x_lin, -SWIGLU_LIMIT, SWIGLU_LIMIT)
    return x_glu * jax.nn.sigmoid(SWIGLU_ALPHA * x_glu) * (x_lin + 1.0)


def moe_ffn(h, w_router, b_router, w_gate_up, b_gate_up, w_down, b_down):
    b, s, d = h.shape
    t = b * s
    hf = h.reshape(t, d)
    logits = (hf @ w_router + b_router).astype(jnp.float32)
    top_vals, top_idx = lax.top_k(logits, TOP_K)
    top_w = jax.nn.softmax(top_vals, axis=-1)
    n_assign = t * TOP_K
    e_flat = top_idx.reshape(-1)
    tok_flat = jnp.repeat(jnp.arange(t, dtype=jnp.int32), TOP_K)
    w_flat = top_w.reshape(-1)
    order = jnp.argsort(e_flat)
    e_sorted, tok_sorted, w_sorted = e_flat[order], tok_flat[order], w_flat[order]
    counts = jnp.bincount(e_flat, length=N_EXPERTS)
    starts = jnp.cumsum(counts) - counts
    padded = ((counts + MOE_BLOCK - 1) // MOE_BLOCK) * MOE_BLOCK
    pad_ends = jnp.cumsum(padded)
    pad_starts = pad_ends - padded
    dest = pad_starts[e_sorted] + (jnp.arange(n_assign) - starts[e_sorted])
    n_blocks = (n_assign + MOE_BLOCK - 1) // MOE_BLOCK + N_EXPERTS
    row_tok = jnp.zeros((n_blocks * MOE_BLOCK,), jnp.int32).at[dest].set(tok_sorted)
    row_w = jnp.zeros((n_blocks * MOE_BLOCK,), w_sorted.dtype).at[dest].set(w_sorted)
    block_e = jnp.clip(jnp.searchsorted(pad_ends, jnp.arange(n_blocks) * MOE_BLOCK, side='right'), 0, N_EXPERTS - 1)

    def expert_block(args):
        tok, wt, e = args
        xb = hf[tok]
        hid = clamped_swiglu(xb @ w_gate_up[e] + b_gate_up[e])
        out = hid @ w_down[e] + b_down[e]
        return out * wt[:, None].astype(out.dtype)

    outs = lax.map(expert_block, (row_tok.reshape(n_blocks, MOE_BLOCK), row_w.reshape(n_blocks, MOE_BLOCK), block_e))
    y = jax.ops.segment_sum(outs.reshape(-1, d), row_tok, num_segments=t)
    return y.reshape(b, s, d)


def setup_inputs(seed: int = 0) -> dict:
    key = jax.random.key(seed)
    ks = jax.random.split(key, 32)
    f32 = jnp.float32

    def nrm(k, shape, scale):
        return jax.random.normal(k, shape, f32) * scale

    D = D_MODEL
    gate_offset = jnp.zeros((6 * D,), f32).at[2 * D:3 * D].set(1.0).at[5 * D:6 * D].set(1.0)
    u = jax.random.uniform(ks[8], (DEPTH, SSM_HEADS), f32)
    dt0 = jnp.exp(u * (math.log(0.1) - math.log(0.001)) + math.log(0.001))
    return {
        'x': nrm(ks[0], (BATCH, SEQ, D), 1.0),
        'c': nrm(ks[1], (BATCH, D), 1.0),
        'w_ada': nrm(ks[2], (DEPTH, D, 6 * D), 0.1 * D ** -0.5),
        'b_ada': nrm(ks[3], (DEPTH, 6 * D), 0.02) + gate_offset,
        'norm1_w': 1.0 + nrm(ks[4], (DEPTH, D), 0.02),
        'w_in': nrm(ks[5], (DEPTH, D, IN_COLS), D ** -0.5),
        'conv_ssm_w': nrm(ks[6], (DEPTH, SSM_CONV, CONV_DIM), SSM_CONV ** -0.5),
        'conv_ssm_b': nrm(ks[7], (DEPTH, CONV_DIM), 0.02),
        'dt_bias': dt0 + jnp.log(-jnp.expm1(-dt0)),
        'a_log': jnp.log(jax.random.uniform(ks[9], (DEPTH, SSM_HEADS), f32, 1.0, 16.0)),
        'd_skip': 1.0 + nrm(ks[10], (DEPTH, SSM_HEADS), 0.02),
        'ssm_norm_w': 1.0 + nrm(ks[11], (DEPTH, D_INNER), 0.02),
        'w_ssm_out': nrm(ks[12], (DEPTH, D_INNER, D), D_INNER ** -0.5),
        'conf_dw_w': nrm(ks[13], (DEPTH, CONF_KERNEL, CONF_DIM), CONF_KERNEL ** -0.5),
        'conf_dw_b': nrm(ks[14], (DEPTH, CONF_DIM), 0.02),
        'conf_ln_w': 1.0 + nrm(ks[15], (DEPTH, CONF_DIM), 0.02),
        'conf_ln_b': nrm(ks[16], (DEPTH, CONF_DIM), 0.02),
        'w_conf_out': nrm(ks[17], (DEPTH, CONF_DIM, D), CONF_DIM ** -0.5),
        'w_o': nrm(ks[18], (DEPTH, D, D), D ** -0.5),
        'norm2_w': 1.0 + nrm(ks[19], (DEPTH, D), 0.02),
        'w_router': nrm(ks[20], (DEPTH, D, N_EXPERTS), D ** -0.5),
        'b_router': nrm(ks[21], (DEPTH, N_EXPERTS), 0.01),
        'w_gate_up': nrm(ks[22], (DEPTH, N_EXPERTS, D, 2 * D_FF), D ** -0.5),
        'b_gate_up': nrm(ks[23], (DEPTH, N_EXPERTS, 2 * D_FF), 0.02),
        'w_down': nrm(ks[24], (DEPTH, N_EXPERTS, D_FF, D), D_FF ** -0.5),
        'b_down': nrm(ks[25], (DEPTH, N_EXPERTS, D), 0.02),
        'final_norm_w': 1.0 + nrm(ks[26], (D,), 0.02),
    }


def reference(x, c, w_ada, b_ada, norm1_w, w_in, conv_ssm_w, conv_ssm_b, dt_bias, a_log, d_skip,
              ssm_norm_w, w_ssm_out, conf_dw_w, conf_dw_b, conf_ln_w, conf_ln_b, w_conf_out, w_o,
              norm2_w, w_router, b_router, w_gate_up, b_gate_up, w_down, b_down, final_norm_w):
    c_act = jax.nn.silu(c)
    for l in range(DEPTH):
        mod = c_act @ w_ada[l] + b_ada[l]
        shift1, scale1, gate1, shift2, scale2, gate2 = jnp.split(mod, 6, axis=-1)
        h = modulate(rms_norm(x, norm1_w[l]), shift1, scale1)
        mix = hybrid_mixer(h, w_in[l], conv_ssm_w[l], conv_ssm_b[l], dt_bias[l], a_log[l], d_skip[l],
                           ssm_norm_w[l], w_ssm_out[l], conf_dw_w[l], conf_dw_b[l], conf_ln_w[l],
                           conf_ln_b[l], w_conf_out[l], w_o[l])
        x = x + gate1[:, None, :] * mix
        h = modulate(rms_norm(x, norm2_w[l]), shift2, scale2)
        ffn = moe_ffn(h, w_router[l], b_router[l], w_gate_up[l], b_gate_up[l], w_down[l], b_down[l])
        x = x + gate2[:, None, :] * ffn
    return rms_norm(x, final_norm_w)
```

```python
import functools

import jax
import jax.numpy as jnp
from jax import lax
from jax.experimental import pallas as pl
from jax.experimental.pallas import tpu as pltpu

F32 = jnp.float32
BF16 = jnp.bfloat16
I32 = jnp.int32

EPS = 1e-6
HEADDIM = 64
NGROUPS = 8
DSTATE = 128
CHUNK = 128
CONV_HALO = 8
CONF_HALO = 32
TOPK = 4
SWIGLU_LIMIT = 7.0
SWIGLU_ALPHA = 1.702
LANES = 128
MOE_ROWS = 256
VMEM_LIMIT = 56 * 1024 * 1024


def _cparams(sem):
    return pltpu.CompilerParams(dimension_semantics=sem, vmem_limit_bytes=VMEM_LIMIT)


def _sigmoid(v):
    return 1.0 / (1.0 + jnp.exp(-v))


def _silu(v):
    return v * _sigmoid(v)


def _split2(a):
    hi = a.astype(BF16)
    lo = (a - hi.astype(F32)).astype(BF16)
    return hi, lo


def _split3(a):
    hi = a.astype(BF16)
    r = a - hi.astype(F32)
    mid = r.astype(BF16)
    lo = (r - mid.astype(F32)).astype(BF16)
    return hi, mid, lo


def _dot(a, b):
    return jnp.dot(a, b, preferred_element_type=F32)


def _dot_nt(a, b):
    return lax.dot_general(a, b, (((1,), (1,)), ((), ())), preferred_element_type=F32)


def _dot_tn(a, b):
    return lax.dot_general(a, b, (((0,), (0,)), ((), ())), preferred_element_type=F32)


def _dot_f32(a, b):
    ah, al = _split2(a)
    bh, bl = _split2(b)
    return _dot(ah, bh) + _dot(ah, bl) + _dot(al, bh)


def _ada_kernel(c_ref, w_ref, b_ref, o_ref):
    c = c_ref[...]
    o_ref[...] = _dot_f32(_silu(c), w_ref[...]) + b_ref[...]


def _ada(c, w_ada, b_ada):
    bsz, d = c.shape
    n = w_ada.shape[1]
    tn = d
    return pl.pallas_call(
        _ada_kernel,
        grid=(n // tn,),
        in_specs=[
            pl.BlockSpec((bsz, d), lambda j: (0, 0)),
            pl.BlockSpec((d, tn), lambda j: (0, j)),
            pl.BlockSpec((1, tn), lambda j: (0, j)),
        ],
        out_specs=pl.BlockSpec((bsz, tn), lambda j: (0, j)),
        out_shape=jax.ShapeDtypeStruct((bsz, n), F32),
        compiler_params=_cparams(("arbitrary",)),
        name="ada",
    )(c, w_ada, b_ada.reshape(1, n))


def _inproj_kernel(x_ref, sh_ref, sc_ref, nw_ref, w_ref, wdt_ref, o_ref, dt_ref, h_scr):
    @pl.when(pl.program_id(2) == 0)
    def _():
        x = x_ref[0]
        ms = jnp.mean(x * x, axis=-1, keepdims=True)
        y = x * lax.rsqrt(ms + EPS) * nw_ref[...]
        hb = (y * (1.0 + sc_ref[0]) + sh_ref[0]).astype(BF16)
        h_scr[...] = hb
        dt_ref[0] = _dot(hb, wdt_ref[...])

    o_ref[0] = _dot(h_scr[...], w_ref[...]).astype(BF16)


def _inproj(x, shift, scale, norm_w, w_main, w_dt, tm, tn):
    bsz, s, d = x.shape
    n = w_main.shape[1]
    return pl.pallas_call(
        _inproj_kernel,
        grid=(bsz, s // tm, n // tn),
        in_specs=[
            pl.BlockSpec((1, tm, d), lambda b, i, j: (b, i, 0)),
            pl.BlockSpec((1, 1, d), lambda b, i, j: (b, 0, 0)),
            pl.BlockSpec((1, 1, d), lambda b, i, j: (b, 0, 0)),
            pl.BlockSpec((1, d), lambda b, i, j: (0, 0)),
            pl.BlockSpec((d, tn), lambda b, i, j: (0, j)),
            pl.BlockSpec((d, LANES), lambda b, i, j: (0, 0)),
        ],
        out_specs=[
            pl.BlockSpec((1, tm, tn), lambda b, i, j: (b, i, j)),
            pl.BlockSpec((1, tm, LANES), lambda b, i, j: (b, i, 0)),
        ],
        out_shape=[
            jax.ShapeDtypeStruct((bsz, s, n), BF16),
            jax.ShapeDtypeStruct((bsz, s, LANES), F32),
        ],
        scratch_shapes=[pltpu.VMEM((tm, d), BF16)],
        compiler_params=_cparams(("arbitrary", "arbitrary", "arbitrary")),
        name="inproj",
    )(x, shift, scale, norm_w, w_main, w_dt)


def _ssd_kernel(z_ref, xs_ref, bc_ref, dt_ref, cw_ref, cb_ref, dtb_ref, a_ref, dsk_ref, nw_ref, wout_ref,
                e_ref, o_ref, ext_scr, xc_scr, state_scr, yn_scr):
    L = CHUNK
    d_inner = xs_ref.shape[2]
    gw = d_inner // NGROUPS
    hpg = gw // HEADDIM
    conv_w = cw_ref.shape[0]

    @pl.when(pl.program_id(1) == 0)
    def _():
        ext_scr[0:CONV_HALO, :] = jnp.zeros((CONV_HALO, ext_scr.shape[1]), F32)
        state_scr[...] = jnp.zeros(state_scr.shape, F32)

    ext_scr[CONV_HALO:CONV_HALO + L, 0:d_inner] = xs_ref[0].astype(F32)
    ext_scr[CONV_HALO:CONV_HALO + L, d_inner:] = bc_ref[0].astype(F32)
    cc = 256
    for j in range(ext_scr.shape[1] // cc):
        cs = slice(j * cc, (j + 1) * cc)
        acc = jnp.broadcast_to(cb_ref[:, cs], (L, cc))
        for k in range(conv_w):
            off = CONV_HALO - (conv_w - 1) + k
            acc = acc + cw_ref[k:k + 1, cs] * ext_scr[off:off + L, cs]
        xc_scr[:, cs] = _silu(acc)
    ext_scr[0:CONV_HALO, :] = ext_scr[L:L + CONV_HALO, :]

    raw = dt_ref[0] + dtb_ref[...]
    dt = jnp.maximum(raw, 0.0) + jnp.log(1.0 + jnp.exp(-jnp.abs(raw)))
    adt = dt * a_ref[...]
    row = lax.broadcasted_iota(I32, (L, L), 0)
    col = lax.broadcasted_iota(I32, (L, L), 1)
    causal = row >= col
    tri = causal.astype(BF16)
    a_hi, a_mid, a_lo = _split3(adt)
    acs = _dot(tri, a_hi) + _dot(tri, a_mid) + _dot(tri, a_lo)
    acs_t = acs.T
    last = acs[L - 1:L, :]
    stack = jnp.concatenate(
        [dt, jnp.exp(acs), jnp.exp(last - acs), jnp.broadcast_to(jnp.exp(last), (8, LANES))], axis=0)
    s_hi, s_lo = _split2(stack)

    for g in range(NGROUPS):
        gs = slice(g * gw, (g + 1) * gw)
        ex = _dot(s_hi, e_ref[:, gs]) + _dot(s_lo, e_ref[:, gs])
        dt_x, ea_x, dte_x, cd_x = ex[0:L], ex[L:2 * L], ex[2 * L:3 * L], ex[3 * L:3 * L + 1]
        xs_g = xc_scr[:, gs]
        b_g = xc_scr[:, d_inner + g * DSTATE:d_inner + (g + 1) * DSTATE].astype(BF16)
        c_g = xc_scr[:, d_inner + NGROUPS * DSTATE + g * DSTATE:d_inner + NGROUPS * DSTATE + (g + 1) * DSTATE].astype(BF16)
        xdt = xs_g * dt_x
        xdt_b = xdt.astype(BF16)
        cb = _dot_nt(c_g, b_g)
        ys = []
        for hh in range(hpg):
            h = g * hpg + hh
            seg = acs[:, h:h + 1] - acs_t[h:h + 1, :]
            dec = jnp.where(causal, jnp.exp(seg), 0.0)
            m = (cb * dec).astype(BF16)
            ys.append(_dot(m, xdt_b[:, hh * HEADDIM:(hh + 1) * HEADDIM]))
        y = jnp.concatenate(ys, axis=1)
        st = state_scr[g]
        y = y + _dot(c_g, st.astype(BF16)) * ea_x
        state_scr[g] = st * cd_x + _dot_tn(b_g, (xdt * dte_x).astype(BF16))
        y = y + dsk_ref[:, gs] * xs_g
        zg = z_ref[0, :, gs].astype(F32)
        y = y * _silu(zg)
        ms = jnp.mean(y * y, axis=-1, keepdims=True)
        yn_scr[:, gs] = (y * lax.rsqrt(ms + EPS) * nw_ref[:, gs]).astype(BF16)

    o_ref[0] = _dot(yn_scr[...], wout_ref[...]).astype(BF16)


def _ssd(proj, dt_raw, conv_w, conv_b, dt_bias, a_neg, d_skip_x, norm_w, w_out, expand, d_inner):
    bsz, s, _ = proj.shape
    d = w_out.shape[1]
    conv_dim = conv_w.shape[1]
    bc_w = conv_dim - d_inner
    assert bc_w == d_inner, "the B|C columns are fetched as one block of d_inner columns"
    L = CHUNK
    const = lambda b, c: (0, 0)
    return pl.pallas_call(
        _ssd_kernel,
        grid=(bsz, s // L),
        in_specs=[
            pl.BlockSpec((1, L, d_inner), lambda b, c: (b, c, 0)),
            pl.BlockSpec((1, L, d_inner), lambda b, c: (b, c, 1)),
            pl.BlockSpec((1, L, d_inner), lambda b, c: (b, c, 2)),
            pl.BlockSpec((1, L, LANES), lambda b, c: (b, c, 0)),
            pl.BlockSpec(conv_w.shape, const),
            pl.BlockSpec((1, conv_dim), const),
            pl.BlockSpec((1, LANES), const),
            pl.BlockSpec((1, LANES), const),
            pl.BlockSpec((1, d_inner), const),
            pl.BlockSpec((1, d_inner), const),
            pl.BlockSpec((d_inner, d), const),
            pl.BlockSpec((LANES, d_inner), const),
        ],
        out_specs=pl.BlockSpec((1, L, d), lambda b, c: (b, c, 0)),
        out_shape=jax.ShapeDtypeStruct((bsz, s, d), BF16),
        scratch_shapes=[
            pltpu.VMEM((L + CONV_HALO, conv_dim), F32),
            pltpu.VMEM((L, conv_dim), F32),
            pltpu.VMEM((NGROUPS, DSTATE, d_inner // NGROUPS), F32),
            pltpu.VMEM((L, d_inner), BF16),
        ],
        compiler_params=_cparams(("arbitrary", "arbitrary")),
        name="ssd",
    )(proj, proj, proj, dt_raw, conv_w, conv_b, dt_bias, a_neg, d_skip_x, norm_w, w_out, expand)


def _conf_kernel(glu_ref, gt_ref, ya_ref, x_ref, g1_ref, sh2_ref, sc2_ref, dw_ref, db_ref, lnw_ref, lnb_ref,
                 wc_ref, wo_ref, n2w_ref, wr_ref, br_ref, x1_ref, h2_ref, lg_ref, ext_scr, cv_scr):
    ts = x_ref.shape[1]
    d = x_ref.shape[2]
    kw = dw_ref.shape[0]

    @pl.when(pl.program_id(1) == 0)
    def _():
        ext_scr[0:CONF_HALO, :] = jnp.zeros((CONF_HALO, d), F32)

    u_val = glu_ref[0, :, 0:d].astype(F32)
    u_gate = glu_ref[0, :, d:2 * d].astype(F32)
    ext_scr[CONF_HALO:CONF_HALO + ts, :] = u_val * _sigmoid(u_gate)
    cc = LANES
    for j in range(d // cc):
        cs = slice(j * cc, (j + 1) * cc)
        acc = jnp.broadcast_to(db_ref[:, cs], (ts, cc))
        for k in range(kw):
            off = CONF_HALO - (kw - 1) + k
            acc = acc + dw_ref[k:k + 1, cs] * ext_scr[off:off + ts, cs]
        cv_scr[:, cs] = acc
    ext_scr[0:CONF_HALO, :] = ext_scr[ts:ts + CONF_HALO, :]

    u = cv_scr[...]
    mu = jnp.mean(u, axis=-1, keepdims=True)
    uc = u - mu
    var = jnp.mean(uc * uc, axis=-1, keepdims=True)
    v = _silu(uc * lax.rsqrt(var + EPS) * lnw_ref[...] + lnb_ref[...])
    y_b = _dot(v.astype(BF16), wc_ref[...])
    g_a = gt_ref[0, :, 0:d].astype(F32)
    g_b = gt_ref[0, :, d:2 * d].astype(F32)
    merged = _sigmoid(g_a) * ya_ref[0].astype(F32) + _sigmoid(g_b) * y_b
    mix = _dot(merged.astype(BF16), wo_ref[...])
    x1 = x_ref[0] + g1_ref[0] * mix
    x1_ref[0] = x1
    ms = jnp.mean(x1 * x1, axis=-1, keepdims=True)
    h2 = (x1 * lax.rsqrt(ms + EPS) * n2w_ref[...]) * (1.0 + sc2_ref[0]) + sh2_ref[0]
    h2_ref[0] = h2
    hh, hl = _split2(h2)
    wh, wl = _split2(wr_ref[...])
    lg_ref[0] = _dot_nt(wh, hh) + _dot_nt(wh, hl) + _dot_nt(wl, hh) + br_ref[...]


def _conf(proj, y_a, x, gate1, shift2, scale2, dw_w, dw_b, ln_w, ln_b, w_conf, w_o, n2w, wr_t, br, ts):
    bsz, s, d = x.shape
    ne = wr_t.shape[0]
    glu_blk = proj.shape[2] // (2 * d) - 2
    const = lambda b, i: (0, 0)
    bvec = pl.BlockSpec((1, 1, d), lambda b, i: (b, 0, 0))
    tile = pl.BlockSpec((1, ts, d), lambda b, i: (b, i, 0))
    return pl.pallas_call(
        _conf_kernel,
        grid=(bsz, s // ts),
        in_specs=[
            pl.BlockSpec((1, ts, 2 * d), lambda b, i: (b, i, glu_blk)),
            pl.BlockSpec((1, ts, 2 * d), lambda b, i: (b, i, glu_blk + 1)),
            tile, tile, bvec, bvec, bvec,
            pl.BlockSpec(dw_w.shape, const),
            pl.BlockSpec((1, d), const), pl.BlockSpec((1, d), const), pl.BlockSpec((1, d), const),
            pl.BlockSpec((d, d), const), pl.BlockSpec((d, d), const),
            pl.BlockSpec((1, d), const),
            pl.BlockSpec((ne, d), const), pl.BlockSpec((ne, 1), const),
        ],
        out_specs=[tile, tile, pl.BlockSpec((1, ne, ts), lambda b, i: (b, 0, i))],
        out_shape=[
            jax.ShapeDtypeStruct((bsz, s, d), F32),
            jax.ShapeDtypeStruct((bsz, s, d), F32),
            jax.ShapeDtypeStruct((bsz, ne, s), F32),
        ],
        scratch_shapes=[pltpu.VMEM((ts + CONF_HALO, d), F32), pltpu.VMEM((ts, d), F32)],
        compiler_params=_cparams(("arbitrary", "arbitrary")),
        name="conf",
    )(proj, proj, y_a, x, gate1, shift2, scale2, dw_w, dw_b, ln_w, ln_b, w_conf, w_o, n2w, wr_t, br)


def _route_kernel(lg_ref, e_ref, w_ref, r_ref, cnt_ref, carry_scr):
    ne, tr = lg_ref.shape[1], lg_ref.shape[2]

    @pl.when((pl.program_id(0) == 0) & (pl.program_id(1) == 0))
    def _():
        carry_scr[...] = jnp.zeros(carry_scr.shape, F32)

    cur = lg_ref[0]
    eid = lax.broadcasted_iota(I32, (ne, tr), 0)
    vals, idxs = [], []
    sel = jnp.zeros((ne, tr), F32)
    for _ in range(TOPK):
        m = jnp.max(cur, axis=0, keepdims=True)
        idx = jnp.min(jnp.where(cur == m, eid, ne), axis=0, keepdims=True)
        hit = eid == idx
        sel = jnp.where(hit, 1.0, sel)
        cur = jnp.where(hit, -jnp.inf, cur)
        vals.append(m)
        idxs.append(idx)
    ex = [jnp.exp(v - vals[0]) for v in vals]
    den = ex[0] + ex[1] + ex[2] + ex[3]
    ut = (lax.broadcasted_iota(I32, (tr, tr), 0) <= lax.broadcasted_iota(I32, (tr, tr), 1)).astype(BF16)
    cum = _dot(sel.astype(BF16), ut)
    before = carry_scr[:, 0:1] + cum - sel
    for k in range(TOPK):
        e_ref[0, k:k + 1, :] = idxs[k]
        w_ref[0, k:k + 1, :] = ex[k] / den
        rk = jnp.sum(jnp.where(eid == idxs[k], before, 0.0), axis=0, keepdims=True)
        r_ref[0, k:k + 1, :] = rk.astype(I32)
    carry_scr[...] = carry_scr[...] + cum[:, tr - 1:tr]
    cnt_ref[...] = carry_scr[...]


def _route(logits_t, tr):
    bsz, ne, s = logits_t.shape
    blk = pl.BlockSpec((1, TOPK, tr), lambda b, i: (b, 0, i))
    return pl.pallas_call(
        _route_kernel,
        grid=(bsz, s // tr),
        in_specs=[pl.BlockSpec((1, ne, tr), lambda b, i: (b, 0, i))],
        out_specs=[blk, blk, blk, pl.BlockSpec((ne, LANES), lambda b, i: (0, 0))],
        out_shape=[
            jax.ShapeDtypeStruct((bsz, TOPK, s), I32),
            jax.ShapeDtypeStruct((bsz, TOPK, s), F32),
            jax.ShapeDtypeStruct((bsz, TOPK, s), I32),
            jax.ShapeDtypeStruct((ne, LANES), F32),
        ],
        scratch_shapes=[pltpu.VMEM((ne, LANES), F32)],
        compiler_params=_cparams(("arbitrary", "arbitrary")),
        name="route",
    )(logits_t)


def _row_copy(src_hbm, dst_vmem, src_row, dst_row, sem):
    return pltpu.make_async_copy(src_hbm.at[pl.ds(src_row, 1), :], dst_vmem.at[pl.ds(dst_row, 1), :], sem)


def _experts_kernel(be_ref, nu_ref, tok_hbm, h_hbm, wg_ref, wl_ref, bg_ref, bl_ref, wd_ref, bd_ref, o_ref,
                    idx_smem, xb, isem, gsem):
    i = pl.program_id(0)
    rows = xb.shape[0]

    @pl.when(i < nu_ref[0])
    def _():
        icp = pltpu.make_async_copy(tok_hbm.at[i], idx_smem, isem)
        icp.start()
        icp.wait()

        def issue(r, c):
            _row_copy(h_hbm, xb, idx_smem[r], r, gsem).start()
            return c

        lax.fori_loop(0, rows, issue, 0)

        def drain(r, c):
            _row_copy(h_hbm, xb, 0, r, gsem).wait()
            return c

        lax.fori_loop(0, rows, drain, 0)

        xbf = xb[...].astype(BF16)
        glu = jnp.minimum(_dot(xbf, wg_ref[0]) + bg_ref[0], SWIGLU_LIMIT)
        lin = jnp.clip(_dot(xbf, wl_ref[0]) + bl_ref[0], -SWIGLU_LIMIT, SWIGLU_LIMIT)
        hid = glu * _sigmoid(SWIGLU_ALPHA * glu) * (lin + 1.0)
        o_ref[...] = _dot(hid.astype(BF16), wd_ref[0]) + bd_ref[0]

    @pl.when(i >= nu_ref[0])
    def _():
        o_ref[...] = jnp.zeros(o_ref.shape, F32)


def _experts(block_e, n_used, row_tok, h2, wg, wl, bg, bl, wd, bd):
    n_blocks, rows = row_tok.shape
    d = h2.shape[1]
    dff = wg.shape[2]
    wspec = lambda shp: pl.BlockSpec((1,) + shp, lambda i, be, nu: (be[i], 0, 0))
    grid_spec = pltpu.PrefetchScalarGridSpec(
        num_scalar_prefetch=2,
        grid=(n_blocks,),
        in_specs=[
            pl.BlockSpec(memory_space=pl.ANY),
            pl.BlockSpec(memory_space=pl.ANY),
            wspec((d, dff)), wspec((d, dff)), wspec((1, dff)), wspec((1, dff)),
            wspec((dff, d)), wspec((1, d)),
        ],
        out_specs=pl.BlockSpec((rows, d), lambda i, be, nu: (i, 0)),
        scratch_shapes=[
            pltpu.SMEM((rows,), I32),
            pltpu.VMEM((rows, d), F32),
            pltpu.SemaphoreType.DMA(()),
            pltpu.SemaphoreType.DMA(()),
        ],
    )
    return pl.pallas_call(
        _experts_kernel,
        grid_spec=grid_spec,
        out_shape=jax.ShapeDtypeStruct((n_blocks * rows, d), F32),
        compiler_params=_cparams(("arbitrary",)),
        name="experts",
    )(block_e, n_used, row_tok, h2, wg, wl, bg, bl, wd, bd)


def _combine_kernel(dest_hbm, outs_hbm, w_ref, x1_ref, g2_ref, fnw_ref, o_ref, idx_smem, buf, isem, gsem):
    i = pl.program_id(0)
    tc = x1_ref.shape[0]
    icp = pltpu.make_async_copy(dest_hbm.at[i], idx_smem, isem)
    icp.start()
    icp.wait()

    def issue(r, c):
        for k in range(TOPK):
            _row_copy(outs_hbm, buf.at[k], idx_smem[r * TOPK + k], r, gsem).start()
        return c

    lax.fori_loop(0, tc, issue, 0)

    def drain(r, c):
        for k in range(TOPK):
            _row_copy(outs_hbm, buf.at[k], 0, r, gsem).wait()
        return c

    lax.fori_loop(0, tc, drain, 0)

    w = w_ref[...]
    y = w[:, 0:1] * buf[0]
    for k in range(1, TOPK):
        y = y + w[:, k:k + 1] * buf[k]
    x2 = x1_ref[...] + g2_ref[0] * y
    ms = jnp.mean(x2 * x2, axis=-1, keepdims=True)
    o_ref[...] = x2 * lax.rsqrt(ms + EPS) * fnw_ref[...]


def _combine(dest_tm, outs, w_tm, x1, gate2, fnw, s, tc):
    t, d = x1.shape
    per_b = s // tc
    return pl.pallas_call(
        _combine_kernel,
        grid=(t // tc,),
        in_specs=[
            pl.BlockSpec(memory_space=pl.ANY),
            pl.BlockSpec(memory_space=pl.ANY),
            pl.BlockSpec((tc, TOPK), lambda i: (i, 0)),
            pl.BlockSpec((tc, d), lambda i: (i, 0)),
            pl.BlockSpec((1, 1, d), lambda i: (i // per_b, 0, 0)),
            pl.BlockSpec((1, d), lambda i: (0, 0)),
        ],
        out_specs=pl.BlockSpec((tc, d), lambda i: (i, 0)),
        out_shape=jax.ShapeDtypeStruct((t, d), F32),
        scratch_shapes=[
            pltpu.SMEM((tc * TOPK,), I32),
            pltpu.VMEM((TOPK, tc, d), F32),
            pltpu.SemaphoreType.DMA(()),
            pltpu.SemaphoreType.DMA(()),
        ],
        compiler_params=_cparams(("arbitrary",)),
        name="combine",
    )(dest_tm, outs, w_tm, x1, gate2, fnw)


def _pick(n, pref):
    return pref if n % pref == 0 else n


def kernel(x, c, w_ada, b_ada, norm1_w, w_in, conv_ssm_w, conv_ssm_b, dt_bias, a_log, d_skip, ssm_norm_w, w_ssm_out, conf_dw_w, conf_dw_b, conf_ln_w, conf_ln_b, w_conf_out, w_o, norm2_w, w_router, b_router, w_gate_up, b_gate_up, w_down, b_down, final_norm_w):
    bsz, s, d = x.shape
    depth = w_ada.shape[0]
    d_inner = w_ssm_out.shape[1]
    heads = dt_bias.shape[1]
    conv_dim = conv_ssm_w.shape[2]
    ne = w_router.shape[2]
    t = bsz * s
    assert s % CHUNK == 0 and heads * HEADDIM == d_inner and heads <= LANES
    assert conv_dim == d_inner + 2 * NGROUPS * DSTATE

    expand = (jnp.arange(LANES)[:, None] == (jnp.arange(d_inner) // HEADDIM)[None, :]).astype(BF16)
    pad_h = lambda v: jnp.pad(v.astype(F32), (0, LANES - heads)).reshape(1, LANES)

    for l in range(depth):
        mod = _ada(c, w_ada[l], b_ada[l])
        shift1, scale1, gate1, shift2, scale2, gate2 = [m.reshape(bsz, 1, d) for m in jnp.split(mod, 6, axis=-1)]

        c0, c1 = d_inner + conv_dim, d_inner + conv_dim + heads
        w_l = w_in[l]
        w_main = jnp.concatenate([w_l[:, :c0], w_l[:, c1:]], axis=1).astype(BF16)
        w_dt = jnp.pad(w_l[:, c0:c1], ((0, 0), (0, LANES - heads))).astype(BF16)
        proj, dt_raw = _inproj(x, shift1, scale1, norm1_w[l].reshape(1, d), w_main, w_dt,
                               _pick(s, 512), _pick(w_main.shape[1], 2048))

        y_a = _ssd(proj, dt_raw, conv_ssm_w[l], conv_ssm_b[l].reshape(1, conv_dim), pad_h(dt_bias[l]),
                   pad_h(-jnp.exp(a_log[l].astype(F32))), jnp.repeat(d_skip[l], HEADDIM).reshape(1, d_inner),
                   ssm_norm_w[l].reshape(1, d_inner), w_ssm_out[l].astype(BF16), expand, d_inner)

        x1, h2, logits_t = _conf(
            proj, y_a, x, gate1, shift2, scale2, conf_dw_w[l], conf_dw_b[l].reshape(1, d),
            conf_ln_w[l].reshape(1, d), conf_ln_b[l].reshape(1, d), w_conf_out[l].astype(BF16),
            w_o[l].astype(BF16), norm2_w[l].reshape(1, d), w_router[l].T, b_router[l].reshape(ne, 1),
            _pick(s, 256))

        e_t, w_t, rank_t, counts = _route(logits_t, _pick(s, 512))

        rows = MOE_ROWS
        n_blocks = (t * TOPK) // rows + ne
        cnt = counts[:, 0].astype(I32)
        padded = ((cnt + rows - 1) // rows) * rows
        pad_end = jnp.cumsum(padded)
        pad_start = pad_end - padded
        dest = pad_start[e_t] + rank_t
        dest_tm = jnp.transpose(dest, (0, 2, 1)).reshape(t * TOPK)
        tok_tm = jnp.repeat(jnp.arange(t, dtype=I32), TOPK)
        row_tok = jnp.zeros((n_blocks * rows,), I32).at[dest_tm].set(tok_tm, unique_indices=True)
        block_e = jnp.clip(jnp.searchsorted(pad_end, jnp.arange(n_blocks, dtype=I32) * rows, side='right'),
                           0, ne - 1).astype(I32)
        n_used = (pad_end[ne - 1:ne] // rows).astype(I32)

        wgu = w_gate_up[l]
        wg = wgu[:, :, 0::2].astype(BF16)
        wl = wgu[:, :, 1::2].astype(BF16)
        bgu = b_gate_up[l]
        dff = wg.shape[2]
        outs = _experts(block_e, n_used, row_tok.reshape(n_blocks, rows), h2.reshape(t, d), wg, wl,
                        bgu[:, 0::2].reshape(ne, 1, dff), bgu[:, 1::2].reshape(ne, 1, dff),
                        w_down[l].astype(BF16), b_down[l].reshape(ne, 1, d))

        tc = _pick(s, 256)
        w_tm = jnp.transpose(w_t, (0, 2, 1)).reshape(t, TOPK)
        last = l == depth - 1
        fnw = final_norm_w.reshape(1, d)
        assert last, "the final norm is fused into the last layer's combine"
        x = _combine(dest_tm.reshape(t // tc, tc * TOPK), outs, w_tm, x1.reshape(t, d), gate2, fnw, s, tc)
        x = x.reshape(bsz, s, d)
    return x
```

```python
import functools

import jax
import jax.numpy as jnp
from jax import lax
from jax.experimental import pallas as pl
from jax.experimental.pallas import tpu as pltpu

F32 = jnp.float32
BF16 = jnp.bfloat16
I32 = jnp.int32

EPS = 1e-6
HEADDIM = 64
NGROUPS = 8
DSTATE = 128
CHUNK = 128
CONV_HALO = 8
CONF_HALO = 32
TOPK = 4
SWIGLU_LIMIT = 7.0
SWIGLU_ALPHA = 1.702
LANES = 128
MOE_ROWS = 256
VMEM_LIMIT = 56 * 1024 * 1024


def _cparams(sem):
    return pltpu.CompilerParams(dimension_semantics=sem, vmem_limit_bytes=VMEM_LIMIT)


def _sigmoid(v):
    return 1.0 / (1.0 + jnp.exp(-v))


def _silu(v):
    return v * _sigmoid(v)


def _split2(a):
    hi = a.astype(BF16)
    lo = (a - hi.astype(F32)).astype(BF16)
    return hi, lo


def _split3(a):
    hi = a.astype(BF16)
    r = a - hi.astype(F32)
    mid = r.astype(BF16)
    lo = (r - mid.astype(F32)).astype(BF16)
    return hi, mid, lo


def _dot(a, b):
    return jnp.dot(a, b, preferred_element_type=F32)


def _dot_nt(a, b):
    return lax.dot_general(a, b, (((1,), (1,)), ((), ())), preferred_element_type=F32)


def _dot_tn(a, b):
    return lax.dot_general(a, b, (((0,), (0,)), ((), ())), preferred_element_type=F32)


def _dot_f32(a, b):
    ah, al = _split2(a)
    bh, bl = _split2(b)
    return _dot(ah, bh) + _dot(ah, bl) + _dot(al, bh)


def _ada_kernel(c_ref, w_ref, b_ref, o_ref):
    c = c_ref[...]
    o_ref[...] = _dot_f32(_silu(c), w_ref[...]) + b_ref[...]


def _ada(c, w_ada, b_ada):
    bsz, d = c.shape
    n = w_ada.shape[1]
    tn = d
    return pl.pallas_call(
        _ada_kernel,
        grid=(n // tn,),
        in_specs=[
            pl.BlockSpec((bsz, d), lambda j: (0, 0)),
            pl.BlockSpec((d, tn), lambda j: (0, j)),
            pl.BlockSpec((1, tn), lambda j: (0, j)),
        ],
        out_specs=pl.BlockSpec((bsz, tn), lambda j: (0, j)),
        out_shape=jax.ShapeDtypeStruct((bsz, n), F32),
        compiler_params=_cparams(("arbitrary",)),
        name="ada",
    )(c, w_ada, b_ada.reshape(1, n))


def _inproj_kernel(x_ref, sh_ref, sc_ref, nw_ref, w_ref, wdt_ref, o_ref, dt_ref, h_scr):
    @pl.when(pl.program_id(2) == 0)
    def _():
        x = x_ref[0]
        ms = jnp.mean(x * x, axis=-1, keepdims=True)
        y = x * lax.rsqrt(ms + EPS) * nw_ref[...]
        hb = (y * (1.0 + sc_ref[0]) + sh_ref[0]).astype(BF16)
        h_scr[...] = hb
        dt_ref[0] = _dot(hb, wdt_ref[...])

    o_ref[0] = _dot(h_scr[...], w_ref[...]).astype(BF16)


def _inproj(x, shift, scale, norm_w, w_main, w_dt, tm, tn):
    bsz, s, d = x.shape
    n = w_main.shape[1]
    return pl.pallas_call(
        _inproj_kernel,
        grid=(bsz, s // tm, n // tn),
        in_specs=[
            pl.BlockSpec((1, tm, d), lambda b, i, j: (b, i, 0)),
            pl.BlockSpec((1, 1, d), lambda b, i, j: (b, 0, 0)),
            pl.BlockSpec((1, 1, d), lambda b, i, j: (b, 0, 0)),
            pl.BlockSpec((1, d), lambda b, i, j: (0, 0)),
            pl.BlockSpec((d, tn), lambda b, i, j: (0, j)),
            pl.BlockSpec((d, LANES), lambda b, i, j: (0, 0)),
        ],
        out_specs=[
            pl.BlockSpec((1, tm, tn), lambda b, i, j: (b, i, j)),
            pl.BlockSpec((1, tm, LANES), lambda b, i, j: (b, i, 0)),
        ],
        out_shape=[
            jax.ShapeDtypeStruct((bsz, s, n), BF16),
            jax.ShapeDtypeStruct((bsz, s, LANES), F32),
        ],
        scratch_shapes=[pltpu.VMEM((tm, d), BF16)],
        compiler_params=_cparams(("arbitrary", "arbitrary", "arbitrary")),
        name="inproj",
    )(x, shift, scale, norm_w, w_main, w_dt)


def _ssd_kernel(z_ref, xs_ref, bc_ref, dt_ref, cw_ref, cb_ref, dtb_ref, a_ref, dsk_ref, nw_ref, wout_ref,
                e_ref, o_ref, ext_scr, xc_scr, state_scr, yn_scr):
    L = CHUNK
    d_inner = xs_ref.shape[2]
    gw = d_inner // NGROUPS
    hpg = gw // HEADDIM
    conv_w = cw_ref.shape[0]

    @pl.when(pl.program_id(1) == 0)
    def _():
        ext_scr[0:CONV_HALO, :] = jnp.zeros((CONV_HALO, ext_scr.shape[1]), F32)
        state_scr[...] = jnp.zeros(state_scr.shape, F32)

    ext_scr[CONV_HALO:CONV_HALO + L, 0:d_inner] = xs_ref[0].astype(F32)
    ext_scr[CONV_HALO:CONV_HALO + L, d_inner:] = bc_ref[0].astype(F32)
    cc = 256
    for j in range(ext_scr.shape[1] // cc):
        cs = slice(j * cc, (j + 1) * cc)
        acc = jnp.broadcast_to(cb_ref[:, cs], (L, cc))
        for k in range(conv_w):
            off = CONV_HALO - (conv_w - 1) + k
            acc = acc + cw_ref[k:k + 1, cs] * ext_scr[off:off + L, cs]
        xc_scr[:, cs] = _silu(acc)
    ext_scr[0:CONV_HALO, :] = ext_scr[L:L + CONV_HALO, :]

    raw = dt_ref[0] + dtb_ref[...]
    dt = jnp.maximum(raw, 0.0) + jnp.log(1.0 + jnp.exp(-jnp.abs(raw)))
    adt = dt * a_ref[...]
    row = lax.broadcasted_iota(I32, (L, L), 0)
    col = lax.broadcasted_iota(I32, (L, L), 1)
    causal = row >= col
    tri = causal.astype(BF16)
    a_hi, a_mid, a_lo = _split3(adt)
    acs = _dot(tri, a_hi) + _dot(tri, a_mid) + _dot(tri, a_lo)
    acs_t = acs.T
    last = acs[L - 1:L, :]
    stack = jnp.concatenate(
        [dt, jnp.exp(acs), jnp.exp(last - acs), jnp.broadcast_to(jnp.exp(last), (8, LANES))], axis=0)
    s_hi, s_lo = _split2(stack)

    for g in range(NGROUPS):
        gs = slice(g * gw, (g + 1) * gw)
        ex = _dot(s_hi, e_ref[:, gs]) + _dot(s_lo, e_ref[:, gs])
        dt_x, ea_x, dte_x, cd_x = ex[0:L], ex[L:2 * L], ex[2 * L:3 * L], ex[3 * L:3 * L + 1]
        xs_g = xc_scr[:, gs]
        b_g = xc_scr[:, d_inner + g * DSTATE:d_inner + (g + 1) * DSTATE].astype(BF16)
        c_g = xc_scr[:, d_inner + NGROUPS * DSTATE + g * DSTATE:d_inner + NGROUPS * DSTATE + (g + 1) * DSTATE].astype(BF16)
        xdt = xs_g * dt_x
        xdt_b = xdt.astype(BF16)
        cb = _dot_nt(c_g, b_g)
        ys = []
        for hh in range(hpg):
            h = g * hpg + hh
            seg = acs[:, h:h + 1] - acs_t[h:h + 1, :]
            dec = jnp.where(causal, jnp.exp(seg), 0.0)
            m = (cb * dec).astype(BF16)
            ys.append(_dot(m, xdt_b[:, hh * HEADDIM:(hh + 1) * HEADDIM]))
        y = jnp.concatenate(ys, axis=1)
        st = state_scr[g]
        y = y + _dot(c_g, st.astype(BF16)) * ea_x
        state_scr[g] = st * cd_x + _dot_tn(b_g, (xdt * dte_x).astype(BF16))
        y = y + dsk_ref[:, gs] * xs_g
        zg = z_ref[0, :, gs].astype(F32)
        y = y * _silu(zg)
        ms = jnp.mean(y * y, axis=-1, keepdims=True)
        yn_scr[:, gs] = (y * lax.rsqrt(ms + EPS) * nw_ref[:, gs]).astype(BF16)

    o_ref[0] = _dot(yn_scr[...], wout_ref[...]).astype(BF16)


def _ssd(proj, dt_raw, conv_w, conv_b, dt_bias, a_neg, d_skip_x, norm_w, w_out, expand, d_inner):
    bsz, s, _ = proj.shape
    d = w_out.shape[1]
    conv_dim = conv_w.shape[1]
    bc_w = conv_dim - d_inner
    assert bc_w == d_inner, "the B|C columns are fetched as one block of d_inner columns"
    L = CHUNK
    const = lambda b, c: (0, 0)
    return pl.pallas_call(
        _ssd_kernel,
        grid=(bsz, s // L),
        in_specs=[
            pl.BlockSpec((1, L, d_inner), lambda b, c: (b, c, 0)),
            pl.BlockSpec((1, L, d_inner), lambda b, c: (b, c, 1)),
            pl.BlockSpec((1, L, d_inner), lambda b, c: (b, c, 2)),
            pl.BlockSpec((1, L, LANES), lambda b, c: (b, c, 0)),
            pl.BlockSpec(conv_w.shape, const),
            pl.BlockSpec((1, conv_dim), const),
            pl.BlockSpec((1, LANES), const),
            pl.BlockSpec((1, LANES), const),
            pl.BlockSpec((1, d_inner), const),
            pl.BlockSpec((1, d_inner), const),
            pl.BlockSpec((d_inner, d), const),
            pl.BlockSpec((LANES, d_inner), const),
        ],
        out_specs=pl.BlockSpec((1, L, d), lambda b, c: (b, c, 0)),
        out_shape=jax.ShapeDtypeStruct((bsz, s, d), BF16),
        scratch_shapes=[
            pltpu.VMEM((L + CONV_HALO, conv_dim), F32),
            pltpu.VMEM((L, conv_dim), F32),
            pltpu.VMEM((NGROUPS, DSTATE, d_inner // NGROUPS), F32),
            pltpu.VMEM((L, d_inner), BF16),
        ],
        compiler_params=_cparams(("arbitrary", "arbitrary")),
        name="ssd",
    )(proj, proj, proj, dt_raw, conv_w, conv_b, dt_bias, a_neg, d_skip_x, norm_w, w_out, expand)


def _conf_kernel(glu_ref, gt_ref, ya_ref, x_ref, g1_ref, sh2_ref, sc2_ref, dw_ref, db_ref, lnw_ref, lnb_ref,
                 wc_ref, wo_ref, n2w_ref, wr_ref, br_ref, x1_ref, h2_ref, lg_ref, ext_scr, cv_scr):
    ts = x_ref.shape[1]
    d = x_ref.shape[2]
    kw = dw_ref.shape[0]

    @pl.when(pl.program_id(1) == 0)
    def _():
        ext_scr[0:CONF_HALO, :] = jnp.zeros((CONF_HALO, d), F32)

    u_val = glu_ref[0, :, 0:d].astype(F32)
    u_gate = glu_ref[0, :, d:2 * d].astype(F32)
    ext_scr[CONF_HALO:CONF_HALO + ts, :] = u_val * _sigmoid(u_gate)
    cc = LANES
    for j in range(d // cc):
        cs = slice(j * cc, (j + 1) * cc)
        acc = jnp.broadcast_to(db_ref[:, cs], (ts, cc))
        for k in range(kw):
            off = CONF_HALO - (kw - 1) + k
            acc = acc + dw_ref[k:k + 1, cs] * ext_scr[off:off + ts, cs]
        cv_scr[:, cs] = acc
    ext_scr[0:CONF_HALO, :] = ext_scr[ts:ts + CONF_HALO, :]

    u = cv_scr[...]
    mu = jnp.mean(u, axis=-1, keepdims=True)
    uc = u - mu
    var = jnp.mean(uc * uc, axis=-1, keepdims=True)
    v = _silu(uc * lax.rsqrt(var + EPS) * lnw_ref[...] + lnb_ref[...])
    y_b = _dot(v.astype(BF16), wc_ref[...])
    g_a = gt_ref[0, :, 0:d].astype(F32)
    g_b = gt_ref[0, :, d:2 * d].astype(F32)
    merged = _sigmoid(g_a) * ya_ref[0].astype(F32) + _sigmoid(g_b) * y_b
    mix = _dot(merged.astype(BF16), wo_ref[...])
    x1 = x_ref[0] + g1_ref[0] * mix
    x1_ref[0] = x1
    ms = jnp.mean(x1 * x1, axis=-1, keepdims=True)
    h2 = (x1 * lax.rsqrt(ms + EPS) * n2w_ref[...]) * (1.0 + sc2_ref[0]) + sh2_ref[0]
    h2_ref[0] = h2
    hh, hl = _split2(h2)
    wh, wl = _split2(wr_ref[...])
    lg_ref[0] = _dot_nt(wh, hh) + _dot_nt(wh, hl) + _dot_nt(wl, hh) + br_ref[...]


def _conf(proj, y_a, x, gate1, shift2, scale2, dw_w, dw_b, ln_w, ln_b, w_conf, w_o, n2w, wr_t, br, ts):
    bsz, s, d = x.shape
    ne = wr_t.shape[0]
    glu_blk = proj.shape[2] // (2 * d) - 2
    const = lambda b, i: (0, 0)
    bvec = pl.BlockSpec((1, 1, d), lambda b, i: (b, 0, 0))
    tile = pl.BlockSpec((1, ts, d), lambda b, i: (b, i, 0))
    return pl.pallas_call(
        _conf_kernel,
        grid=(bsz, s // ts),
        in_specs=[
            pl.BlockSpec((1, ts, 2 * d), lambda b, i: (b, i, glu_blk)),
            pl.BlockSpec((1, ts, 2 * d), lambda b, i: (b, i, glu_blk + 1)),
            tile, tile, bvec, bvec, bvec,
            pl.BlockSpec(dw_w.shape, const),
            pl.BlockSpec((1, d), const), pl.BlockSpec((1, d), const), pl.BlockSpec((1, d), const),
            pl.BlockSpec((d, d), const), pl.BlockSpec((d, d), const),
            pl.BlockSpec((1, d), const),
            pl.BlockSpec((ne, d), const), pl.BlockSpec((ne, 1), const),
        ],
        out_specs=[tile, tile, pl.BlockSpec((1, ne, ts), lambda b, i: (b, 0, i))],
        out_shape=[
            jax.ShapeDtypeStruct((bsz, s, d), F32),
            jax.ShapeDtypeStruct((bsz, s, d), F32),
            jax.ShapeDtypeStruct((bsz, ne, s), F32),
        ],
        scratch_shapes=[pltpu.VMEM((ts + CONF_HALO, d), F32), pltpu.VMEM((ts, d), F32)],
        compiler_params=_cparams(("arbitrary", "arbitrary")),
        name="conf",
    )(proj, proj, y_a, x, gate1, shift2, scale2, dw_w, dw_b, ln_w, ln_b, w_conf, w_o, n2w, wr_t, br)


def _route_kernel(lg_ref, e_ref, w_ref, r_ref, cnt_ref, carry_scr):
    ne, tr = lg_ref.shape[1], lg_ref.shape[2]

    @pl.when((pl.program_id(0) == 0) & (pl.program_id(1) == 0))
    def _():
        carry_scr[...] = jnp.zeros(carry_scr.shape, F32)

    cur = lg_ref[0]
    eid = lax.broadcasted_iota(I32, (ne, tr), 0)
    vals, idxs = [], []
    sel = jnp.zeros((ne, tr), F32)
    for _ in range(TOPK):
        m = jnp.max(cur, axis=0, keepdims=True)
        idx = jnp.min(jnp.where(cur == m, eid, ne), axis=0, keepdims=True)
        hit = eid == idx
        sel = jnp.where(hit, 1.0, sel)
        cur = jnp.where(hit, -jnp.inf, cur)
        vals.append(m)
        idxs.append(idx)
    ex = [jnp.exp(v - vals[0]) for v in vals]
    den = ex[0] + ex[1] + ex[2] + ex[3]
    ut = (lax.broadcasted_iota(I32, (tr, tr), 0) <= lax.broadcasted_iota(I32, (tr, tr), 1)).astype(BF16)
    cum = _dot(sel.astype(BF16), ut)
    before = carry_scr[:, 0:1] + cum - sel
    for k in range(TOPK):
        e_ref[0, k:k + 1, :] = idxs[k]
        w_ref[0, k:k + 1, :] = ex[k] / den
        rk = jnp.sum(jnp.where(eid == idxs[k], before, 0.0), axis=0, keepdims=True)
        r_ref[0, k:k + 1, :] = rk.astype(I32)
    carry_scr[...] = carry_scr[...] + cum[:, tr - 1:tr]
    cnt_ref[...] = carry_scr[...]


def _route(logits_t, tr):
    bsz, ne, s = logits_t.shape
    blk = pl.BlockSpec((1, TOPK, tr), lambda b, i: (b, 0, i))
    return pl.pallas_call(
        _route_kernel,
        grid=(bsz, s // tr),
        in_specs=[pl.BlockSpec((1, ne, tr), lambda b, i: (b, 0, i))],
        out_specs=[blk, blk, blk, pl.BlockSpec((ne, LANES), lambda b, i: (0, 0))],
        out_shape=[
            jax.ShapeDtypeStruct((bsz, TOPK, s), I32),
            jax.ShapeDtypeStruct((bsz, TOPK, s), F32),
            jax.ShapeDtypeStruct((bsz, TOPK, s), I32),
            jax.ShapeDtypeStruct((ne, LANES), F32),
        ],
        scratch_shapes=[pltpu.VMEM((ne, LANES), F32)],
        compiler_params=_cparams(("arbitrary", "arbitrary")),
        name="route",
    )(logits_t)


def _wprep_kernel(w_ref, wg_ref, wl_ref):
    pw = 2 * LANES
    r = lax.broadcasted_iota(I32, (pw, pw), 0)
    c = lax.broadcasted_iota(I32, (pw, pw), 1)
    perm = (r == jnp.where(c < LANES, 2 * c, 2 * (c - LANES) + 1)).astype(BF16)
    for j in range(w_ref.shape[2] // pw):
        blk = _dot(w_ref[0, :, j * pw:(j + 1) * pw].astype(BF16), perm)
        wg_ref[0, :, j * LANES:(j + 1) * LANES] = blk[:, :LANES].astype(BF16)
        wl_ref[0, :, j * LANES:(j + 1) * LANES] = blk[:, LANES:].astype(BF16)


def _wprep(w_gate_up, tm):
    ne, d, n2 = w_gate_up.shape
    out = pl.BlockSpec((1, tm, n2 // 2), lambda e, i: (e, i, 0))
    return pl.pallas_call(
        _wprep_kernel,
        grid=(ne, d // tm),
        in_specs=[pl.BlockSpec((1, tm, n2), lambda e, i: (e, i, 0))],
        out_specs=[out, out],
        out_shape=[jax.ShapeDtypeStruct((ne, d, n2 // 2), BF16)] * 2,
        compiler_params=_cparams(("arbitrary", "arbitrary")),
        name="wprep",
    )(w_gate_up)


def _row_copy(src_hbm, dst_vmem, src_row, dst_row, sem):
    return pltpu.make_async_copy(src_hbm.at[pl.ds(src_row, 1), :], dst_vmem.at[pl.ds(dst_row, 1), :], sem)


def _experts_kernel(be_ref, nu_ref, tok_hbm, h_hbm, wg_ref, wl_ref, bg_ref, bl_ref, wd_ref, bd_ref, o_ref,
                    idx_smem, xb, isem, gsem):
    i = pl.program_id(0)
    rows = xb.shape[0]

    @pl.when(i < nu_ref[0])
    def _():
        icp = pltpu.make_async_copy(tok_hbm.at[i], idx_smem, isem)
        icp.start()
        icp.wait()

        def issue(r, c):
            _row_copy(h_hbm, xb, idx_smem[r], r, gsem).start()
            return c

        lax.fori_loop(0, rows, issue, 0)

        def drain(r, c):
            _row_copy(h_hbm, xb, 0, r, gsem).wait()
            return c

        lax.fori_loop(0, rows, drain, 0)

        xbf = xb[...].astype(BF16)
        glu = jnp.minimum(_dot(xbf, wg_ref[0]) + bg_ref[0], SWIGLU_LIMIT)
        lin = jnp.clip(_dot(xbf, wl_ref[0]) + bl_ref[0], -SWIGLU_LIMIT, SWIGLU_LIMIT)
        hid = glu * _sigmoid(SWIGLU_ALPHA * glu) * (lin + 1.0)
        o_ref[...] = _dot(hid.astype(BF16), wd_ref[0]) + bd_ref[0]

    @pl.when(i >= nu_ref[0])
    def _():
        o_ref[...] = jnp.zeros(o_ref.shape, F32)


def _experts(block_e, n_used, row_tok, h2, wg, wl, bg, bl, wd, bd):
    n_blocks, rows = row_tok.shape
    d = h2.shape[1]
    dff = wg.shape[2]
    wspec = lambda shp: pl.BlockSpec((1,) + shp, lambda i, be, nu: (be[i], 0, 0))
    grid_spec = pltpu.PrefetchScalarGridSpec(
        num_scalar_prefetch=2,
        grid=(n_blocks,),
        in_specs=[
            pl.BlockSpec(memory_space=pl.ANY),
            pl.BlockSpec(memory_space=pl.ANY),
            wspec((d, dff)), wspec((d, dff)), wspec((1, dff)), wspec((1, dff)),
            wspec((dff, d)), wspec((1, d)),
        ],
        out_specs=pl.BlockSpec((rows, d), lambda i, be, nu: (i, 0)),
        scratch_shapes=[
            pltpu.SMEM((rows,), I32),
            pltpu.VMEM((rows, d), F32),
            pltpu.SemaphoreType.DMA(()),
            pltpu.SemaphoreType.DMA(()),
        ],
    )
    return pl.pallas_call(
        _experts_kernel,
        grid_spec=grid_spec,
        out_shape=jax.ShapeDtypeStruct((n_blocks * rows, d), F32),
        compiler_params=_cparams(("arbitrary",)),
        name="experts",
    )(block_e, n_used, row_tok, h2, wg, wl, bg, bl, wd, bd)


def _combine_kernel(dest_hbm, outs_hbm, w_ref, x1_ref, g2_ref, fnw_ref, o_ref, idx_smem, buf, isem, gsem):
    i = pl.program_id(0)
    tc = x1_ref.shape[0]
    icp = pltpu.make_async_copy(dest_hbm.at[i], idx_smem, isem)
    icp.start()
    icp.wait()

    def issue(r, c):
        for k in range(TOPK):
            _row_copy(outs_hbm, buf.at[k], idx_smem[r * TOPK + k], r, gsem).start()
        return c

    lax.fori_loop(0, tc, issue, 0)

    def drain(r, c):
        for k in range(TOPK):
            _row_copy(outs_hbm, buf.at[k], 0, r, gsem).wait()
        return c

    lax.fori_loop(0, tc, drain, 0)

    w = w_ref[...]
    y = w[:, 0:1] * buf[0]
    for k in range(1, TOPK):
        y = y + w[:, k:k + 1] * buf[k]
    x2 = x1_ref[...] + g2_ref[0] * y
    ms = jnp.mean(x2 * x2, axis=-1, keepdims=True)
    o_ref[...] = x2 * lax.rsqrt(ms + EPS) * fnw_ref[...]


def _combine(dest_tm, outs, w_tm, x1, gate2, fnw, s, tc):
    t, d = x1.shape
    per_b = s // tc
    return pl.pallas_call(
        _combine_kernel,
        grid=(t // tc,),
        in_specs=[
            pl.BlockSpec(memory_space=pl.ANY),
            pl.BlockSpec(memory_space=pl.ANY),
            pl.BlockSpec((tc, TOPK), lambda i: (i, 0)),
            pl.BlockSpec((tc, d), lambda i: (i, 0)),
            pl.BlockSpec((1, 1, d), lambda i: (i // per_b, 0, 0)),
            pl.BlockSpec((1, d), lambda i: (0, 0)),
        ],
        out_specs=pl.BlockSpec((tc, d), lambda i: (i, 0)),
        out_shape=jax.ShapeDtypeStruct((t, d), F32),
        scratch_shapes=[
            pltpu.SMEM((tc * TOPK,), I32),
            pltpu.VMEM((TOPK, tc, d), F32),
            pltpu.SemaphoreType.DMA(()),
            pltpu.SemaphoreType.DMA(()),
        ],
        compiler_params=_cparams(("arbitrary",)),
        name="combine",
    )(dest_tm, outs, w_tm, x1, gate2, fnw)


def _pick(n, pref):
    return pref if n % pref == 0 else n


def kernel(x, c, w_ada, b_ada, norm1_w, w_in, conv_ssm_w, conv_ssm_b, dt_bias, a_log, d_skip, ssm_norm_w, w_ssm_out, conf_dw_w, conf_dw_b, conf_ln_w, conf_ln_b, w_conf_out, w_o, norm2_w, w_router, b_router, w_gate_up, b_gate_up, w_down, b_down, final_norm_w):
    bsz, s, d = x.shape
    depth = w_ada.shape[0]
    d_inner = w_ssm_out.shape[1]
    heads = dt_bias.shape[1]
    conv_dim = conv_ssm_w.shape[2]
    ne = w_router.shape[2]
    t = bsz * s
    assert s % CHUNK == 0 and heads * HEADDIM == d_inner and heads <= LANES
    assert conv_dim == d_inner + 2 * NGROUPS * DSTATE

    expand = (jnp.arange(LANES)[:, None] == (jnp.arange(d_inner) // HEADDIM)[None, :]).astype(BF16)
    pad_h = lambda v: jnp.pad(v.astype(F32), (0, LANES - heads)).reshape(1, LANES)

    for l in range(depth):
        mod = _ada(c, w_ada[l], b_ada[l])
        shift1, scale1, gate1, shift2, scale2, gate2 = [m.reshape(bsz, 1, d) for m in jnp.split(mod, 6, axis=-1)]

        c0, c1 = d_inner + conv_dim, d_inner + conv_dim + heads
        w_l = w_in[l]
        w_main = jnp.concatenate([w_l[:, :c0], w_l[:, c1:]], axis=1).astype(BF16)
        w_dt = jnp.pad(w_l[:, c0:c1], ((0, 0), (0, LANES - heads))).astype(BF16)
        proj, dt_raw = _inproj(x, shift1, scale1, norm1_w[l].reshape(1, d), w_main, w_dt,
                               _pick(s, 512), _pick(w_main.shape[1], 2048))

        y_a = _ssd(proj, dt_raw, conv_ssm_w[l], conv_ssm_b[l].reshape(1, conv_dim), pad_h(dt_bias[l]),
                   pad_h(-jnp.exp(a_log[l].astype(F32))), jnp.repeat(d_skip[l], HEADDIM).reshape(1, d_inner),
                   ssm_norm_w[l].reshape(1, d_inner), w_ssm_out[l].astype(BF16), expand, d_inner)

        x1, h2, logits_t = _conf(
            proj, y_a, x, gate1, shift2, scale2, conf_dw_w[l], conf_dw_b[l].reshape(1, d),
            conf_ln_w[l].reshape(1, d), conf_ln_b[l].reshape(1, d), w_conf_out[l].astype(BF16),
            w_o[l].astype(BF16), norm2_w[l].reshape(1, d), w_router[l].T, b_router[l].reshape(ne, 1),
            _pick(s, 256))

        e_t, w_t, rank_t, counts = _route(logits_t, _pick(s, 512))

        rows = MOE_ROWS
        n_blocks = (t * TOPK) // rows + ne
        cnt = counts[:, 0].astype(I32)
        padded = ((cnt + rows - 1) // rows) * rows
        pad_end = jnp.cumsum(padded)
        pad_start = pad_end - padded
        eids = jnp.arange(ne, dtype=I32)
        dest = rank_t + jnp.sum(jnp.where(e_t[..., None] == eids, pad_start, 0), axis=-1)
        dest_tm = jnp.transpose(dest, (0, 2, 1)).reshape(t * TOPK)
        tok_tm = jnp.repeat(jnp.arange(t, dtype=I32), TOPK)
        row_tok = jnp.zeros((n_blocks * rows,), I32).at[dest_tm].set(tok_tm, unique_indices=True)
        blk_start = jnp.arange(n_blocks, dtype=I32) * rows
        block_e = jnp.minimum(jnp.sum((pad_end[None, :] <= blk_start[:, None]).astype(I32), axis=1), ne - 1)
        n_used = (pad_end[ne - 1:ne] // rows).astype(I32)

        wg, wl = _wprep(w_gate_up[l], _pick(d, 512))
        bgu = b_gate_up[l]
        dff = wg.shape[2]
        outs = _experts(block_e, n_used, row_tok.reshape(n_blocks, rows), h2.reshape(t, d), wg, wl,
                        bgu[:, 0::2].reshape(ne, 1, dff), bgu[:, 1::2].reshape(ne, 1, dff),
                        w_down[l].astype(BF16), b_down[l].reshape(ne, 1, d))

        tc = _pick(s, 256)
        w_tm = jnp.transpose(w_t, (0, 2, 1)).reshape(t, TOPK)
        last = l == depth - 1
        fnw = final_norm_w.reshape(1, d)
        assert last, "the final norm is fused into the last layer's combine"
        x = _combine(dest_tm.reshape(t // tc, tc * TOPK), outs, w_tm, x1.reshape(t, d), gate2, fnw, s, tc)
        x = x.reshape(bsz, s, d)
    return x
```

```python
import functools

import jax
import jax.numpy as jnp
from jax import lax
from jax.experimental import pallas as pl
from jax.experimental.pallas import tpu as pltpu

F32 = jnp.float32
BF16 = jnp.bfloat16
I32 = jnp.int32

EPS = 1e-6
HEADDIM = 64
NGROUPS = 8
DSTATE = 128
CHUNK = 128
CONV_HALO = 8
CONF_HALO = 32
TOPK = 4
SWIGLU_LIMIT = 7.0
SWIGLU_ALPHA = 1.702
LANES = 128
MOE_ROWS = 512
MOE_TILE = 256
VMEM_LIMIT = 56 * 1024 * 1024


def _cparams(sem):
    return pltpu.CompilerParams(dimension_semantics=sem, vmem_limit_bytes=VMEM_LIMIT)


def _sigmoid(v):
    return 0.5 * jnp.tanh(0.5 * v) + 0.5


def _silu(v):
    return v * _sigmoid(v)


def _split2(a):
    hi = a.astype(BF16)
    lo = (a - hi.astype(F32)).astype(BF16)
    return hi, lo


def _split3(a):
    hi = a.astype(BF16)
    r = a - hi.astype(F32)
    mid = r.astype(BF16)
    lo = (r - mid.astype(F32)).astype(BF16)
    return hi, mid, lo


def _dot(a, b):
    return jnp.dot(a, b, preferred_element_type=F32)


def _dot_nt(a, b):
    return lax.dot_general(a, b, (((1,), (1,)), ((), ())), preferred_element_type=F32)


def _dot_tn(a, b):
    return lax.dot_general(a, b, (((0,), (0,)), ((), ())), preferred_element_type=F32)


def _dot_f32(a, b):
    ah, al = _split2(a)
    bh, bl = _split2(b)
    return _dot(ah, bh) + _dot(ah, bl) + _dot(al, bh)


def _ada_kernel(c_ref, w_ref, b_ref, o_ref):
    c = c_ref[...]
    o_ref[...] = _dot_f32(_silu(c), w_ref[...]) + b_ref[...]


def _ada(c, w_ada, b_ada):
    bsz, d = c.shape
    n = w_ada.shape[1]
    tn = d
    return pl.pallas_call(
        _ada_kernel,
        grid=(n // tn,),
        in_specs=[
            pl.BlockSpec((bsz, d), lambda j: (0, 0)),
            pl.BlockSpec((d, tn), lambda j: (0, j)),
            pl.BlockSpec((1, tn), lambda j: (0, j)),
        ],
        out_specs=pl.BlockSpec((bsz, tn), lambda j: (0, j)),
        out_shape=jax.ShapeDtypeStruct((bsz, n), F32),
        compiler_params=_cparams(("arbitrary",)),
        name="ada",
    )(c, w_ada, b_ada.reshape(1, n))


def _inproj_kernel(x_ref, sh_ref, sc_ref, nw_ref, w_ref, wdt_ref, o_ref, dt_ref, h_scr):
    @pl.when(pl.program_id(2) == 0)
    def _():
        x = x_ref[0]
        ms = jnp.mean(x * x, axis=-1, keepdims=True)
        y = x * lax.rsqrt(ms + EPS) * nw_ref[...]
        hb = (y * (1.0 + sc_ref[0]) + sh_ref[0]).astype(BF16)
        h_scr[...] = hb
        dt_ref[0] = _dot(hb, wdt_ref[...])

    o_ref[0] = _dot(h_scr[...], w_ref[...]).astype(BF16)


def _inproj(x, shift, scale, norm_w, w_main, w_dt, tm, tn):
    bsz, s, d = x.shape
    n = w_main.shape[1]
    return pl.pallas_call(
        _inproj_kernel,
        grid=(bsz, s // tm, n // tn),
        in_specs=[
            pl.BlockSpec((1, tm, d), lambda b, i, j: (b, i, 0)),
            pl.BlockSpec((1, 1, d), lambda b, i, j: (b, 0, 0)),
            pl.BlockSpec((1, 1, d), lambda b, i, j: (b, 0, 0)),
            pl.BlockSpec((1, d), lambda b, i, j: (0, 0)),
            pl.BlockSpec((d, tn), lambda b, i, j: (0, j)),
            pl.BlockSpec((d, LANES), lambda b, i, j: (0, 0)),
        ],
        out_specs=[
            pl.BlockSpec((1, tm, tn), lambda b, i, j: (b, i, j)),
            pl.BlockSpec((1, tm, LANES), lambda b, i, j: (b, i, 0)),
        ],
        out_shape=[
            jax.ShapeDtypeStruct((bsz, s, n), BF16),
            jax.ShapeDtypeStruct((bsz, s, LANES), F32),
        ],
        scratch_shapes=[pltpu.VMEM((tm, d), BF16)],
        compiler_params=_cparams(("arbitrary", "arbitrary", "arbitrary")),
        name="inproj",
    )(x, shift, scale, norm_w, w_main, w_dt)


def _ssd_kernel(z_ref, xs_ref, bc_ref, dt_ref, cw_ref, cb_ref, dtb_ref, a_ref, dsk_ref, nw_ref, wout_ref,
                e_ref, o_ref, ext_scr, xc_scr, state_scr, yn_scr):
    L = CHUNK
    d_inner = xs_ref.shape[2]
    gw = d_inner // NGROUPS
    hpg = gw // HEADDIM
    conv_w = cw_ref.shape[0]

    @pl.when(pl.program_id(1) == 0)
    def _():
        ext_scr[0:CONV_HALO, :] = jnp.zeros((CONV_HALO, ext_scr.shape[1]), F32)
        state_scr[...] = jnp.zeros(state_scr.shape, F32)

    ext_scr[CONV_HALO:CONV_HALO + L, 0:d_inner] = xs_ref[0].astype(F32)
    ext_scr[CONV_HALO:CONV_HALO + L, d_inner:] = bc_ref[0].astype(F32)
    cc = 256
    for j in range(ext_scr.shape[1] // cc):
        cs = slice(j * cc, (j + 1) * cc)
        acc = jnp.broadcast_to(cb_ref[:, cs], (L, cc))
        for k in range(conv_w):
            off = CONV_HALO - (conv_w - 1) + k
            acc = acc + cw_ref[k:k + 1, cs] * ext_scr[off:off + L, cs]
        xc_scr[:, cs] = _silu(acc)
    ext_scr[0:CONV_HALO, :] = ext_scr[L:L + CONV_HALO, :]

    raw = dt_ref[0] + dtb_ref[...]
    dt = jnp.maximum(raw, 0.0) + jnp.log(1.0 + jnp.exp(-jnp.abs(raw)))
    adt = dt * a_ref[...]
    row = lax.broadcasted_iota(I32, (L, L), 0)
    col = lax.broadcasted_iota(I32, (L, L), 1)
    causal = row >= col
    tri = causal.astype(BF16)
    a_hi, a_mid, a_lo = _split3(adt)
    acs = _dot(tri, a_hi) + _dot(tri, a_mid) + _dot(tri, a_lo)
    acs_t = acs.T
    last = acs[L - 1:L, :]
    stack = jnp.concatenate(
        [dt, jnp.exp(acs), jnp.exp(last - acs), jnp.broadcast_to(jnp.exp(last), (8, LANES))], axis=0)
    s_hi, s_lo = _split2(stack)

    for g in range(NGROUPS):
        gs = slice(g * gw, (g + 1) * gw)
        ex = _dot(s_hi, e_ref[:, gs]) + _dot(s_lo, e_ref[:, gs])
        dt_x, ea_x, dte_x, cd_x = ex[0:L], ex[L:2 * L], ex[2 * L:3 * L], ex[3 * L:3 * L + 1]
        xs_g = xc_scr[:, gs]
        b_g = xc_scr[:, d_inner + g * DSTATE:d_inner + (g + 1) * DSTATE].astype(BF16)
        c_g = xc_scr[:, d_inner + NGROUPS * DSTATE + g * DSTATE:d_inner + NGROUPS * DSTATE + (g + 1) * DSTATE].astype(BF16)
        xdt = xs_g * dt_x
        xdt_b = xdt.astype(BF16)
        cb = _dot_nt(c_g, b_g)
        ys = []
        for hh in range(hpg):
            h = g * hpg + hh
            seg = acs[:, h:h + 1] - acs_t[h:h + 1, :]
            dec = jnp.where(causal, jnp.exp(seg), 0.0)
            m = (cb * dec).astype(BF16)
            ys.append(_dot(m, xdt_b[:, hh * HEADDIM:(hh + 1) * HEADDIM]))
        y = jnp.concatenate(ys, axis=1)
        st = state_scr[g]
        y = y + _dot(c_g, st.astype(BF16)) * ea_x
        state_scr[g] = st * cd_x + _dot_tn(b_g, (xdt * dte_x).astype(BF16))
        y = y + dsk_ref[:, gs] * xs_g
        zg = z_ref[0, :, gs].astype(F32)
        y = y * _silu(zg)
        ms = jnp.mean(y * y, axis=-1, keepdims=True)
        yn_scr[:, gs] = (y * lax.rsqrt(ms + EPS) * nw_ref[:, gs]).astype(BF16)

    o_ref[0] = _dot(yn_scr[...], wout_ref[...]).astype(BF16)


def _ssd(proj, dt_raw, conv_w, conv_b, dt_bias, a_neg, d_skip_x, norm_w, w_out, expand, d_inner):
    bsz, s, _ = proj.shape
    d = w_out.shape[1]
    conv_dim = conv_w.shape[1]
    bc_w = conv_dim - d_inner
    assert bc_w == d_inner, "the B|C columns are fetched as one block of d_inner columns"
    L = CHUNK
    const = lambda b, c: (0, 0)
    return pl.pallas_call(
        _ssd_kernel,
        grid=(bsz, s // L),
        in_specs=[
            pl.BlockSpec((1, L, d_inner), lambda b, c: (b, c, 0)),
            pl.BlockSpec((1, L, d_inner), lambda b, c: (b, c, 1)),
            pl.BlockSpec((1, L, d_inner), lambda b, c: (b, c, 2)),
            pl.BlockSpec((1, L, LANES), lambda b, c: (b, c, 0)),
            pl.BlockSpec(conv_w.shape, const),
            pl.BlockSpec((1, conv_dim), const),
            pl.BlockSpec((1, LANES), const),
            pl.BlockSpec((1, LANES), const),
            pl.BlockSpec((1, d_inner), const),
            pl.BlockSpec((1, d_inner), const),
            pl.BlockSpec((d_inner, d), const),
            pl.BlockSpec((LANES, d_inner), const),
        ],
        out_specs=pl.BlockSpec((1, L, d), lambda b, c: (b, c, 0)),
        out_shape=jax.ShapeDtypeStruct((bsz, s, d), BF16),
        scratch_shapes=[
            pltpu.VMEM((L + CONV_HALO, conv_dim), F32),
            pltpu.VMEM((L, conv_dim), F32),
            pltpu.VMEM((NGROUPS, DSTATE, d_inner // NGROUPS), F32),
            pltpu.VMEM((L, d_inner), BF16),
        ],
        compiler_params=_cparams(("arbitrary", "arbitrary")),
        name="ssd",
    )(proj, proj, proj, dt_raw, conv_w, conv_b, dt_bias, a_neg, d_skip_x, norm_w, w_out, expand)


def _conf_kernel(glu_ref, gt_ref, ya_ref, x_ref, g1_ref, sh2_ref, sc2_ref, dw_ref, db_ref, lnw_ref, lnb_ref,
                 wc_ref, wo_ref, n2w_ref, wr_ref, br_ref, x1_ref, h2_ref, lg_ref, ext_scr, rot_scr, cv_scr):
    ts = x_ref.shape[1]
    d = x_ref.shape[2]
    kw = dw_ref.shape[0]
    sub = 8
    rl = rot_scr.shape[1]

    @pl.when(pl.program_id(1) == 0)
    def _():
        ext_scr[0:CONF_HALO, :] = jnp.zeros((CONF_HALO, d), F32)

    u_val = glu_ref[0, :, 0:d].astype(F32)
    u_gate = glu_ref[0, :, d:2 * d].astype(F32)
    ext_scr[CONF_HALO:CONF_HALO + ts, :] = u_val * _sigmoid(u_gate)
    for r in range(1, sub):
        rot_scr[r - 1] = ext_scr[r:r + rl, :]
    cc = LANES
    for j in range(d // cc):
        cs = slice(j * cc, (j + 1) * cc)
        acc = jnp.broadcast_to(db_ref[:, cs], (ts, cc))
        for k in range(kw):
            off = CONF_HALO - (kw - 1) + k
            r = off % sub
            base = off - r
            tap = ext_scr[base:base + ts, cs] if r == 0 else rot_scr[r - 1, base:base + ts, cs]
            acc = acc + dw_ref[k:k + 1, cs] * tap
        cv_scr[:, cs] = acc
    ext_scr[0:CONF_HALO, :] = ext_scr[ts:ts + CONF_HALO, :]

    u = cv_scr[...]
    mu = jnp.mean(u, axis=-1, keepdims=True)
    uc = u - mu
    var = jnp.mean(uc * uc, axis=-1, keepdims=True)
    v = _silu(uc * lax.rsqrt(var + EPS) * lnw_ref[...] + lnb_ref[...])
    y_b = _dot(v.astype(BF16), wc_ref[...])
    g_a = gt_ref[0, :, 0:d].astype(F32)
    g_b = gt_ref[0, :, d:2 * d].astype(F32)
    merged = _sigmoid(g_a) * ya_ref[0].astype(F32) + _sigmoid(g_b) * y_b
    mix = _dot(merged.astype(BF16), wo_ref[...])
    x1 = x_ref[0] + g1_ref[0] * mix
    x1_ref[0] = x1
    ms = jnp.mean(x1 * x1, axis=-1, keepdims=True)
    h2 = (x1 * lax.rsqrt(ms + EPS) * n2w_ref[...]) * (1.0 + sc2_ref[0]) + sh2_ref[0]
    h2_ref[0] = h2.astype(BF16)
    hh, hl = _split2(h2)
    wh, wl = _split2(wr_ref[...])
    lg_ref[0] = _dot_nt(wh, hh) + _dot_nt(wh, hl) + _dot_nt(wl, hh) + br_ref[...]


def _conf(proj, y_a, x, gate1, shift2, scale2, dw_w, dw_b, ln_w, ln_b, w_conf, w_o, n2w, wr_t, br, ts):
    bsz, s, d = x.shape
    ne = wr_t.shape[0]
    glu_blk = proj.shape[2] // (2 * d) - 2
    const = lambda b, i: (0, 0)
    bvec = pl.BlockSpec((1, 1, d), lambda b, i: (b, 0, 0))
    tile = pl.BlockSpec((1, ts, d), lambda b, i: (b, i, 0))
    return pl.pallas_call(
        _conf_kernel,
        grid=(bsz, s // ts),
        in_specs=[
            pl.BlockSpec((1, ts, 2 * d), lambda b, i: (b, i, glu_blk)),
            pl.BlockSpec((1, ts, 2 * d), lambda b, i: (b, i, glu_blk + 1)),
            tile, tile, bvec, bvec, bvec,
            pl.BlockSpec(dw_w.shape, const),
            pl.BlockSpec((1, d), const), pl.BlockSpec((1, d), const), pl.BlockSpec((1, d), const),
            pl.BlockSpec((d, d), const), pl.BlockSpec((d, d), const),
            pl.BlockSpec((1, d), const),
            pl.BlockSpec((ne, d), const), pl.BlockSpec((ne, 1), const),
        ],
        out_specs=[tile, tile, pl.BlockSpec((1, ne, ts), lambda b, i: (b, 0, i))],
        out_shape=[
            jax.ShapeDtypeStruct((bsz, s, d), F32),
            jax.ShapeDtypeStruct((bsz, s, d), BF16),
            jax.ShapeDtypeStruct((bsz, ne, s), F32),
        ],
        scratch_shapes=[pltpu.VMEM((ts + CONF_HALO, d), F32), pltpu.VMEM((7, ts + CONF_HALO - 8, d), F32),
                        pltpu.VMEM((ts, d), F32)],
        compiler_params=_cparams(("arbitrary", "arbitrary")),
        name="conf",
    )(proj, proj, y_a, x, gate1, shift2, scale2, dw_w, dw_b, ln_w, ln_b, w_conf, w_o, n2w, wr_t, br)


def _route_kernel(lg_ref, w_ref, p_ref, cnt_ref):
    ne, tt = lg_ref.shape[1], lg_ref.shape[2]
    cur = lg_ref[0]
    eid = lax.broadcasted_iota(I32, (ne, tt), 0)
    vals, idxs = [], []
    sel = jnp.zeros((ne, tt), F32)
    for _ in range(TOPK):
        m = jnp.max(cur, axis=0, keepdims=True)
        idx = jnp.min(jnp.where(cur == m, eid, ne), axis=0, keepdims=True)
        hit = eid == idx
        sel = jnp.where(hit, 1.0, sel)
        cur = jnp.where(hit, -jnp.inf, cur)
        vals.append(m)
        idxs.append(idx)
    ex = [jnp.exp(v - vals[0]) for v in vals]
    den = ex[0] + ex[1] + ex[2] + ex[3]
    ut = (lax.broadcasted_iota(I32, (tt, tt), 0) <= lax.broadcasted_iota(I32, (tt, tt), 1)).astype(BF16)
    cum = _dot(sel.astype(BF16), ut)
    tot = jnp.broadcast_to(cum[:, tt - 1:tt], (ne, LANES))
    below = (lax.broadcasted_iota(I32, (ne, ne), 0) > lax.broadcasted_iota(I32, (ne, ne), 1)).astype(BF16)
    first = _dot(below, tot.astype(BF16))[:, 0:1]
    slot = first + cum - sel
    for k in range(TOPK):
        w_ref[0, k:k + 1, :] = ex[k] / den
        pk = jnp.sum(jnp.where(eid == idxs[k], slot, 0.0), axis=0, keepdims=True)
        p_ref[0, k:k + 1, :] = pk.astype(I32)
    cnt_ref[0, 0] = tot


def _route(logits_t, tt):
    bsz, ne, s = logits_t.shape
    assert tt <= 256, "tile counts must stay exactly representable in bf16"
    blk = pl.BlockSpec((1, TOPK, tt), lambda b, i: (b, 0, i))
    return pl.pallas_call(
        _route_kernel,
        grid=(bsz, s // tt),
        in_specs=[pl.BlockSpec((1, ne, tt), lambda b, i: (b, 0, i))],
        out_specs=[blk, blk, pl.BlockSpec((1, 1, ne, LANES), lambda b, i: (b, i, 0, 0))],
        out_shape=[
            jax.ShapeDtypeStruct((bsz, TOPK, s), F32),
            jax.ShapeDtypeStruct((bsz, TOPK, s), I32),
            jax.ShapeDtypeStruct((bsz, s // tt, ne, LANES), F32),
        ],
        compiler_params=_cparams(("arbitrary", "arbitrary")),
        name="route",
    )(logits_t)


def _wprep_kernel(w_ref, wg_ref, wl_ref):
    pw = 2 * LANES
    r = lax.broadcasted_iota(I32, (pw, pw), 0)
    c = lax.broadcasted_iota(I32, (pw, pw), 1)
    perm = (r == jnp.where(c < LANES, 2 * c, 2 * (c - LANES) + 1)).astype(BF16)
    for j in range(w_ref.shape[2] // pw):
        blk = _dot(w_ref[0, :, j * pw:(j + 1) * pw].astype(BF16), perm)
        wg_ref[0, :, j * LANES:(j + 1) * LANES] = blk[:, :LANES].astype(BF16)
        wl_ref[0, :, j * LANES:(j + 1) * LANES] = blk[:, LANES:].astype(BF16)


def _wprep(w_gate_up, tm):
    ne, d, n2 = w_gate_up.shape
    out = pl.BlockSpec((1, tm, n2 // 2), lambda e, i: (e, i, 0))
    return pl.pallas_call(
        _wprep_kernel,
        grid=(ne, d // tm),
        in_specs=[pl.BlockSpec((1, tm, n2), lambda e, i: (e, i, 0))],
        out_specs=[out, out],
        out_shape=[jax.ShapeDtypeStruct((ne, d, n2 // 2), BF16)] * 2,
        compiler_params=_cparams(("arbitrary", "arbitrary")),
        name="wprep",
    )(w_gate_up)


SLAB = 8


def _copy_run(src, dst, src_tok, dst_tok, n, sem, nbits):
    done = jnp.int32(0)
    for j in reversed(range(nbits)):
        sz = 1 << j
        s0 = pl.multiple_of((src_tok + done) * SLAB, SLAB)
        d0 = pl.multiple_of((dst_tok + done) * SLAB, SLAB)

        @pl.when((n & sz) != 0)
        def _():
            pltpu.make_async_copy(src.at[pl.ds(s0, sz * SLAB), :], dst.at[pl.ds(d0, sz * SLAB), :], sem).start()

        done = done + (n & sz)


def _dispatch_kernel(rb_ref, rc_ref, re_ref, nu_ref, h_ref, p_ref, xs_hbm, slab, zbuf, sem, zsem, *, rows):
    i = pl.program_id(0)
    last = pl.num_programs(0) - 1
    tt, d = h_ref.shape
    a = tt * TOPK
    ne = re_ref.shape[0]

    def all_runs_landed():
        pltpu.make_async_copy(slab, xs_hbm.at[pl.ds(0, a * SLAB), :], sem).wait()

    @pl.when(i == 0)
    def _():
        zbuf[...] = jnp.zeros(zbuf.shape, F32)
        zero = lambda tok: pltpu.make_async_copy(
            zbuf, xs_hbm.at[pl.ds(pl.multiple_of(tok * SLAB, SLAB), rows * SLAB), :], zsem)
        for e in range(ne):
            zero(re_ref[e]).start()
        for e in range(ne):
            zero(0).wait()
        n_alloc = xs_hbm.shape[0] // (rows * SLAB)
        for j in range(ne + 1):
            @pl.when(nu_ref[0] + j < n_alloc)
            def _():
                zero((nu_ref[0] + j) * rows).start()
        for j in range(ne + 1):
            @pl.when(nu_ref[0] + j < n_alloc)
            def _():
                zero(0).wait()

    pos = p_ref[0]
    rid = lax.broadcasted_iota(I32, (a, tt), 0)
    hit = rid == pos[0:1]
    for k in range(1, TOPK):
        hit = hit | (rid == pos[k:k + 1])
    onehot = hit.astype(BF16)

    @pl.when(i > 0)
    def _():
        all_runs_landed()

    cw = 2 * LANES
    for j in range(d // cw):
        blk = _dot(onehot, h_ref[:, j * cw:(j + 1) * cw])
        slab[pl.ds(2 * j, a, stride=SLAB), :] = blk[:, :LANES]
        slab[pl.ds(2 * j + 1, a, stride=SLAB), :] = blk[:, LANES:]

    def body(e, off):
        n = rc_ref[i * ne + e]
        _copy_run(slab, xs_hbm, off, rb_ref[i * ne + e], n, sem, tt.bit_length())
        return off + n

    lax.fori_loop(0, ne, body, jnp.int32(0))

    @pl.when(i == last)
    def _():
        all_runs_landed()


def _dispatch(run_base, run_cnt, real_end, n_used, h2, pos_t, n_rows, rows, tt):
    t, d = h2.shape
    bsz, _, s = pos_t.shape
    per_b = s // tt
    assert d == SLAB * LANES
    grid_spec = pltpu.PrefetchScalarGridSpec(
        num_scalar_prefetch=4,
        grid=(t // tt,),
        in_specs=[
            pl.BlockSpec((tt, d), lambda i, rb, rc, re, nu: (i, 0)),
            pl.BlockSpec((1, TOPK, tt), lambda i, rb, rc, re, nu: (i // per_b, 0, i % per_b)),
        ],
        out_specs=pl.BlockSpec(memory_space=pl.ANY),
        scratch_shapes=[
            pltpu.VMEM((tt * TOPK * SLAB, LANES), F32),
            pltpu.VMEM((rows * SLAB, LANES), F32),
            pltpu.SemaphoreType.DMA(()),
            pltpu.SemaphoreType.DMA(()),
        ],
    )
    return pl.pallas_call(
        functools.partial(_dispatch_kernel, rows=rows),
        grid_spec=grid_spec,
        out_shape=jax.ShapeDtypeStruct((n_rows * SLAB, LANES), F32),
        compiler_params=_cparams(("arbitrary",)),
        name="dispatch",
    )(run_base, run_cnt, real_end, n_used, h2, pos_t)


def _experts_kernel(be_ref, nu_ref, x_ref, wg_ref, wl_ref, bg_ref, bl_ref, wd_ref, bd_ref, o_ref):
    i = pl.program_id(0)
    rows = x_ref.shape[0] // SLAB

    @pl.when(i < nu_ref[0])
    def _():
        xbf = jnp.concatenate([x_ref[pl.ds(j, rows, stride=SLAB), :] for j in range(SLAB)], axis=1).astype(BF16)
        glu = jnp.minimum(_dot(xbf, wg_ref[0]) + bg_ref[0], SWIGLU_LIMIT)
        lin = jnp.clip(_dot(xbf, wl_ref[0]) + bl_ref[0], -SWIGLU_LIMIT, SWIGLU_LIMIT)
        hid = glu * _sigmoid(SWIGLU_ALPHA * glu) * (lin + 1.0)
        out = _dot(hid.astype(BF16), wd_ref[0]) + bd_ref[0]
        for j in range(SLAB):
            o_ref[pl.ds(j, rows, stride=SLAB), :] = out[:, j * LANES:(j + 1) * LANES]

    @pl.when(i >= nu_ref[0])
    def _():
        o_ref[...] = jnp.zeros(o_ref.shape, F32)


def _experts(block_e, n_used, x_sorted, wg, wl, bg, bl, wd, bd, rows):
    n_blocks = block_e.shape[0]
    d, dff = wg.shape[1], wg.shape[2]
    wspec = lambda shp: pl.BlockSpec((1,) + shp, lambda i, be, nu: (be[i], 0, 0))
    used = lambda i, be, nu: (jnp.minimum(i, nu[0] - 1), 0)
    grid_spec = pltpu.PrefetchScalarGridSpec(
        num_scalar_prefetch=2,
        grid=(n_blocks,),
        in_specs=[
            pl.BlockSpec((rows * SLAB, LANES), used),
            wspec((d, dff)), wspec((d, dff)), wspec((1, dff)), wspec((1, dff)),
            wspec((dff, d)), wspec((1, d)),
        ],
        out_specs=pl.BlockSpec((rows * SLAB, LANES), lambda i, be, nu: (i, 0)),
    )
    return pl.pallas_call(
        _experts_kernel,
        grid_spec=grid_spec,
        out_shape=jax.ShapeDtypeStruct((n_blocks * rows * SLAB, LANES), F32),
        compiler_params=_cparams(("arbitrary",)),
        name="experts",
    )(block_e, n_used, x_sorted, wg, wl, bg, bl, wd, bd)


def _combine_kernel(rb_ref, rc_ref, outs_hbm, p_ref, w_ref, x1_ref, g2_ref, fnw_ref, o_ref, stage, sems):
    i = pl.program_id(0)
    n_tiles = pl.num_programs(0)
    tt, d = x1_ref.shape
    a = tt * TOPK
    ne = rc_ref.shape[0] // n_tiles

    def fetch(tile, slot):
        def body(e, off):
            n = rc_ref[tile * ne + e]
            _copy_run(outs_hbm, stage.at[slot], rb_ref[tile * ne + e], off, n, sems.at[slot], tt.bit_length())
            return off + n

        lax.fori_loop(0, ne, body, jnp.int32(0))

    slot = i % 2

    @pl.when(i == 0)
    def _():
        fetch(i, slot)

    @pl.when(i + 1 < n_tiles)
    def _():
        fetch(i + 1, 1 - slot)

    pltpu.make_async_copy(outs_hbm.at[pl.ds(0, a * SLAB), :], stage.at[slot], sems.at[slot]).wait()

    pos = p_ref[...]
    w = w_ref[...]
    cid = lax.broadcasted_iota(I32, (tt, a), 1)
    pw = jnp.where(cid == pos[:, 0:1], w[:, 0:1], 0.0)
    for k in range(1, TOPK):
        pw = pw + jnp.where(cid == pos[:, k:k + 1], w[:, k:k + 1], 0.0)
    pw_hi, pw_lo = _split2(pw)
    ys = []
    for j in range(0, SLAB, 2):
        so = jnp.concatenate([stage[slot, pl.ds(j, a, stride=SLAB), :],
                              stage[slot, pl.ds(j + 1, a, stride=SLAB), :]], axis=1).astype(BF16)
        ys.append(_dot(pw_hi, so) + _dot(pw_lo, so))
    y = jnp.concatenate(ys, axis=1)
    x2 = x1_ref[...] + g2_ref[0] * y
    ms = jnp.mean(x2 * x2, axis=-1, keepdims=True)
    o_ref[...] = x2 * lax.rsqrt(ms + EPS) * fnw_ref[...]


def _combine(run_base, run_cnt, outs, pos_tm, w_tm, x1, gate2, fnw, s, tt):
    t, d = x1.shape
    per_b = s // tt
    grid_spec = pltpu.PrefetchScalarGridSpec(
        num_scalar_prefetch=2,
        grid=(t // tt,),
        in_specs=[
            pl.BlockSpec(memory_space=pl.ANY),
            pl.BlockSpec((tt, TOPK), lambda i, rb, rc: (i, 0)),
            pl.BlockSpec((tt, TOPK), lambda i, rb, rc: (i, 0)),
            pl.BlockSpec((tt, d), lambda i, rb, rc: (i, 0)),
            pl.BlockSpec((1, 1, d), lambda i, rb, rc: (i // per_b, 0, 0)),
            pl.BlockSpec((1, d), lambda i, rb, rc: (0, 0)),
        ],
        out_specs=pl.BlockSpec((tt, d), lambda i, rb, rc: (i, 0)),
        scratch_shapes=[
            pltpu.VMEM((2, tt * TOPK * SLAB, LANES), F32),
            pltpu.SemaphoreType.DMA((2,)),
        ],
    )
    return pl.pallas_call(
        _combine_kernel,
        grid_spec=grid_spec,
        out_shape=jax.ShapeDtypeStruct((t, d), F32),
        compiler_params=_cparams(("arbitrary",)),
        name="combine",
    )(run_base, run_cnt, outs, pos_tm, w_tm, x1, gate2, fnw)


def _pick(n, pref):
    return pref if n % pref == 0 else n


def kernel(x, c, w_ada, b_ada, norm1_w, w_in, conv_ssm_w, conv_ssm_b, dt_bias, a_log, d_skip, ssm_norm_w, w_ssm_out, conf_dw_w, conf_dw_b, conf_ln_w, conf_ln_b, w_conf_out, w_o, norm2_w, w_router, b_router, w_gate_up, b_gate_up, w_down, b_down, final_norm_w):
    bsz, s, d = x.shape
    depth = w_ada.shape[0]
    d_inner = w_ssm_out.shape[1]
    heads = dt_bias.shape[1]
    conv_dim = conv_ssm_w.shape[2]
    ne = w_router.shape[2]
    t = bsz * s
    assert s % CHUNK == 0 and heads * HEADDIM == d_inner and heads <= LANES
    assert conv_dim == d_inner + 2 * NGROUPS * DSTATE
    assert depth == 1, "the final norm is fused into the (single) layer's combine"

    expand = (jnp.arange(LANES)[:, None] == (jnp.arange(d_inner) // HEADDIM)[None, :]).astype(BF16)
    pad_h = lambda v: jnp.pad(v.astype(F32), (0, LANES - heads)).reshape(1, LANES)

    l = 0
    mod = _ada(c, w_ada[l], b_ada[l])
    shift1, scale1, gate1, shift2, scale2, gate2 = [m.reshape(bsz, 1, d) for m in jnp.split(mod, 6, axis=-1)]

    c0, c1 = d_inner + conv_dim, d_inner + conv_dim + heads
    w_l = w_in[l]
    w_main = jnp.concatenate([w_l[:, :c0], w_l[:, c1:]], axis=1).astype(BF16)
    w_dt = jnp.pad(w_l[:, c0:c1], ((0, 0), (0, LANES - heads))).astype(BF16)
    proj, dt_raw = _inproj(x, shift1, scale1, norm1_w[l].reshape(1, d), w_main, w_dt,
                           _pick(s, 512), _pick(w_main.shape[1], 2048))

    y_a = _ssd(proj, dt_raw, conv_ssm_w[l], conv_ssm_b[l].reshape(1, conv_dim), pad_h(dt_bias[l]),
               pad_h(-jnp.exp(a_log[l].astype(F32))), jnp.repeat(d_skip[l], HEADDIM).reshape(1, d_inner),
               ssm_norm_w[l].reshape(1, d_inner), w_ssm_out[l].astype(BF16), expand, d_inner)

    x1, h2, logits_t = _conf(
        proj, y_a, x, gate1, shift2, scale2, conf_dw_w[l], conf_dw_b[l].reshape(1, d),
        conf_ln_w[l].reshape(1, d), conf_ln_b[l].reshape(1, d), w_conf_out[l].astype(BF16),
        w_o[l].astype(BF16), norm2_w[l].reshape(1, d), w_router[l].T, b_router[l].reshape(ne, 1),
        _pick(s, 256))

    tt = _pick(s, MOE_TILE)
    n_tiles = t // tt
    w_t, pos_t, tcnt = _route(logits_t, tt)

    rows = MOE_ROWS
    n_blocks = (t * TOPK) // rows + ne
    tcnt = tcnt[..., 0].reshape(n_tiles, ne).astype(I32)
    counts = jnp.sum(tcnt, axis=0)
    padded = ((counts + rows - 1) // rows) * rows
    pad_end = jnp.cumsum(padded)
    pad_start = pad_end - padded
    run_base = (pad_start[None, :] + jnp.cumsum(tcnt, axis=0) - tcnt).reshape(-1)
    run_cnt = tcnt.reshape(-1)
    real_end = pad_start + counts
    blk_start = jnp.arange(n_blocks, dtype=I32) * rows
    block_e = jnp.minimum(jnp.sum((pad_end[None, :] <= blk_start[:, None]).astype(I32), axis=1), ne - 1)
    n_used = (pad_end[ne - 1:ne] // rows).astype(I32)

    x_sorted = _dispatch(run_base, run_cnt, real_end, n_used, h2.reshape(t, d), pos_t, (n_blocks + 1) * rows, rows, tt)

    wg, wl = _wprep(w_gate_up[l], _pick(d, 512))
    bgu = b_gate_up[l]
    dff = wg.shape[2]
    outs = _experts(block_e, n_used, x_sorted, wg, wl,
                    bgu[:, 0::2].reshape(ne, 1, dff), bgu[:, 1::2].reshape(ne, 1, dff),
                    w_down[l].astype(BF16), b_down[l].reshape(ne, 1, d), rows)

    pos_tm = jnp.transpose(pos_t, (0, 2, 1)).reshape(t, TOPK)
    w_tm = jnp.transpose(w_t, (0, 2, 1)).reshape(t, TOPK)
    out = _combine(run_base, run_cnt, outs, pos_tm, w_tm, x1.reshape(t, d), gate2, final_norm_w.reshape(1, d), s, tt)
    return out.reshape(bsz, s, d)
```

```python
import functools

import jax
import jax.numpy as jnp
from jax import lax
from jax.experimental import pallas as pl
from jax.experimental.pallas import tpu as pltpu

F32 = jnp.float32
BF16 = jnp.bfloat16
I32 = jnp.int32

EPS = 1e-6
HEADDIM = 64
NGROUPS = 8
DSTATE = 128
CHUNK = 128
CONV_HALO = 8
CONF_HALO = 32
TOPK = 4
SWIGLU_LIMIT = 7.0
SWIGLU_ALPHA = 1.702
LANES = 128
MOE_ROWS = 512
MOE_TILE = 256
PROJ_COLS = 512
VMEM_LIMIT = 56 * 1024 * 1024


def _cparams(sem):
    return pltpu.CompilerParams(dimension_semantics=sem, vmem_limit_bytes=VMEM_LIMIT)


def _sigmoid(v):
    return 0.5 * jnp.tanh(0.5 * v) + 0.5


def _silu(v):
    return v * _sigmoid(v)


def _split2(a):
    hi = a.astype(BF16)
    lo = (a - hi.astype(F32)).astype(BF16)
    return hi, lo


def _split3(a):
    hi = a.astype(BF16)
    r = a - hi.astype(F32)
    mid = r.astype(BF16)
    lo = (r - mid.astype(F32)).astype(BF16)
    return hi, mid, lo


def _dot(a, b):
    return jnp.dot(a, b, preferred_element_type=F32)


def _dot_nt(a, b):
    return lax.dot_general(a, b, (((1,), (1,)), ((), ())), preferred_element_type=F32)


def _dot_tn(a, b):
    return lax.dot_general(a, b, (((0,), (0,)), ((), ())), preferred_element_type=F32)


def _dot_f32(a, b):
    ah, al = _split2(a)
    bh, bl = _split2(b)
    return _dot(ah, bh) + _dot(ah, bl) + _dot(al, bh)


def _ada_kernel(c_ref, w_ref, b_ref, o_ref):
    c = c_ref[...]
    o_ref[...] = _dot_f32(_silu(c), w_ref[...]) + b_ref[...]


def _ada(c, w_ada, b_ada):
    bsz, d = c.shape
    n = w_ada.shape[1]
    tn = d
    return pl.pallas_call(
        _ada_kernel,
        grid=(n // tn,),
        in_specs=[
            pl.BlockSpec((bsz, d), lambda j: (0, 0)),
            pl.BlockSpec((d, tn), lambda j: (0, j)),
            pl.BlockSpec((1, tn), lambda j: (0, j)),
        ],
        out_specs=pl.BlockSpec((bsz, tn), lambda j: (0, j)),
        out_shape=jax.ShapeDtypeStruct((bsz, n), F32),
        compiler_params=_cparams(("arbitrary",)),
        name="ada",
    )(c, w_ada, b_ada.reshape(1, n))


def _norm_mod(x, nw, sc, sh):
    ms = jnp.mean(x * x, axis=-1, keepdims=True)
    return ((x * lax.rsqrt(ms + EPS) * nw) * (1.0 + sc) + sh).astype(BF16)


def _project_piece(h_ref, w_ref, dst_ref, j, cw):
    dst_ref[:, j * cw:(j + 1) * cw] = _dot(h_ref[...], w_ref[:, j * cw:(j + 1) * cw]).astype(BF16)


def _project(h_ref, w_ref, dst_ref, cw):
    for j in range(w_ref.shape[1] // cw):
        _project_piece(h_ref, w_ref, dst_ref, j, cw)


def _ssd_kernel(x0_ref, xn_ref, sh0_ref, sc0_ref, shn_ref, scn_ref, n1w_ref, wzx_ref, wdt_ref,
                cw_ref, cb_ref, dtb_ref, a_ref, dsk_ref, nw_ref, wout_ref, e_ref, o_ref,
                pcur_scr, pnxt_scr, dcur_scr, dnxt_scr, hn_scr, ext_scr, xc_scr, state_scr, yn_scr, *, per_b):
    L = CHUNK
    ts = xn_ref.shape[0]
    d_inner = yn_scr.shape[1]
    gw = d_inner // NGROUPS
    hpg = gw // HEADDIM
    conv_w = cw_ref.shape[0]
    i = pl.program_id(0)

    @pl.when(i == 0)
    def _():
        hn_scr[...] = _norm_mod(x0_ref[...], n1w_ref[...], sc0_ref[0], sh0_ref[0])
        _project(hn_scr, wzx_ref, pcur_scr, PROJ_COLS)
        dcur_scr[...] = _dot(hn_scr[...], wdt_ref[...])

    hn_scr[...] = _norm_mod(xn_ref[...], n1w_ref[...], scn_ref[0], shn_ref[0])
    pieces = [functools.partial(_project_piece, hn_scr, wzx_ref, pnxt_scr, j, PROJ_COLS)
              for j in range(wzx_ref.shape[1] // PROJ_COLS)]

    def dt_piece():
        dnxt_scr[...] = _dot(hn_scr[...], wdt_ref[...])

    pieces.append(dt_piece)

    @pl.when(i % per_b == 0)
    def _():
        ext_scr[0:CONV_HALO, :] = jnp.zeros((CONV_HALO, ext_scr.shape[1]), F32)
        state_scr[...] = jnp.zeros(state_scr.shape, F32)

    row = lax.broadcasted_iota(I32, (L, L), 0)
    col = lax.broadcasted_iota(I32, (L, L), 1)
    causal = row >= col
    tri = causal.astype(BF16)

    for c in range(ts // L):
        rs = slice(c * L, (c + 1) * L)
        ext_scr[CONV_HALO:CONV_HALO + L, :] = pcur_scr[rs, d_inner:].astype(F32)
        cc = 256
        for j in range(ext_scr.shape[1] // cc):
            cs = slice(j * cc, (j + 1) * cc)
            acc = jnp.broadcast_to(cb_ref[:, cs], (L, cc))
            for k in range(conv_w):
                off = CONV_HALO - (conv_w - 1) + k
                acc = acc + cw_ref[k:k + 1, cs] * ext_scr[off:off + L, cs]
            xc_scr[:, cs] = _silu(acc)
        ext_scr[0:CONV_HALO, :] = ext_scr[L:L + CONV_HALO, :]

        raw = dcur_scr[rs, :] + dtb_ref[...]
        dt = jnp.maximum(raw, 0.0) + jnp.log(1.0 + jnp.exp(-jnp.abs(raw)))
        adt = dt * a_ref[...]
        a_hi, a_mid, a_lo = _split3(adt)
        acs = _dot(tri, a_hi) + _dot(tri, a_mid) + _dot(tri, a_lo)
        acs_t = acs.T
        last = acs[L - 1:L, :]
        stack = jnp.concatenate(
            [dt, jnp.exp(acs), jnp.exp(last - acs), jnp.broadcast_to(jnp.exp(last), (8, LANES))], axis=0)
        s_hi, s_lo = _split2(stack)

        for g in range(NGROUPS):
            if pieces:
                pieces.pop(0)()
            gs = slice(g * gw, (g + 1) * gw)
            ex = _dot(s_hi, e_ref[:, gs]) + _dot(s_lo, e_ref[:, gs])
            dt_x, ea_x, dte_x, cd_x = ex[0:L], ex[L:2 * L], ex[2 * L:3 * L], ex[3 * L:3 * L + 1]
            xs_g = xc_scr[:, gs]
            b_g = xc_scr[:, d_inner + g * DSTATE:d_inner + (g + 1) * DSTATE].astype(BF16)
            c_g = xc_scr[:, d_inner + NGROUPS * DSTATE + g * DSTATE:d_inner + NGROUPS * DSTATE + (g + 1) * DSTATE].astype(BF16)
            xdt = xs_g * dt_x
            xdt_b = xdt.astype(BF16)
            cb = _dot_nt(c_g, b_g)
            ys = []
            for hh in range(hpg):
                h = g * hpg + hh
                seg = acs[:, h:h + 1] - acs_t[h:h + 1, :]
                dec = jnp.where(causal, jnp.exp(seg), 0.0)
                m = (cb * dec).astype(BF16)
                ys.append(_dot(m, xdt_b[:, hh * HEADDIM:(hh + 1) * HEADDIM]))
            y = jnp.concatenate(ys, axis=1)
            st = state_scr[g]
            y = y + _dot(c_g, st.astype(BF16)) * ea_x
            state_scr[g] = st * cd_x + _dot_tn(b_g, (xdt * dte_x).astype(BF16))
            y = y + dsk_ref[:, gs] * xs_g
            zg = pcur_scr[rs, gs].astype(F32)
            y = y * _silu(zg)
            ms = jnp.mean(y * y, axis=-1, keepdims=True)
            yn_scr[rs, gs] = (y * lax.rsqrt(ms + EPS) * nw_ref[:, gs]).astype(BF16)

    while pieces:
        pieces.pop(0)()
    o_ref[...] = _dot(yn_scr[...], wout_ref[...]).astype(BF16)
    pcur_scr[...] = pnxt_scr[...]
    dcur_scr[...] = dnxt_scr[...]


def _ssd(x2d, shift, scale, n1w, w_zx, w_dt, conv_w, conv_b, dt_bias, a_neg, d_skip_x, norm_w, w_out, expand,
         d_inner, s, ts):
    t, d = x2d.shape
    conv_dim = conv_w.shape[1]
    n_zx = w_zx.shape[1]
    assert n_zx == d_inner + conv_dim
    per_b = s // ts
    n_tiles = t // ts
    const = lambda i: (0, 0)
    nxt = lambda i: jnp.minimum(i + 1, n_tiles - 1)
    bvec = lambda f: pl.BlockSpec((1, 1, d), lambda i: (f(i) // per_b, 0, 0))
    return pl.pallas_call(
        functools.partial(_ssd_kernel, per_b=per_b),
        grid=(n_tiles,),
        in_specs=[
            pl.BlockSpec((ts, d), const),
            pl.BlockSpec((ts, d), lambda i: (nxt(i), 0)),
            bvec(lambda i: 0 * i), bvec(lambda i: 0 * i), bvec(nxt), bvec(nxt),
            pl.BlockSpec((1, d), const),
            pl.BlockSpec((d, n_zx), const),
            pl.BlockSpec((d, LANES), const),
            pl.BlockSpec(conv_w.shape, const),
            pl.BlockSpec((1, conv_dim), const),
            pl.BlockSpec((1, LANES), const),
            pl.BlockSpec((1, LANES), const),
            pl.BlockSpec((1, d_inner), const),
            pl.BlockSpec((1, d_inner), const),
            pl.BlockSpec((d_inner, d), const),
            pl.BlockSpec((LANES, d_inner), const),
        ],
        out_specs=pl.BlockSpec((ts, d), lambda i: (i, 0)),
        out_shape=jax.ShapeDtypeStruct((t, d), BF16),
        scratch_shapes=[
            pltpu.VMEM((ts, n_zx), BF16), pltpu.VMEM((ts, n_zx), BF16),
            pltpu.VMEM((ts, LANES), F32), pltpu.VMEM((ts, LANES), F32),
            pltpu.VMEM((ts, d), BF16),
            pltpu.VMEM((CHUNK + CONV_HALO, conv_dim), F32),
            pltpu.VMEM((CHUNK, conv_dim), F32),
            pltpu.VMEM((NGROUPS, DSTATE, d_inner // NGROUPS), F32),
            pltpu.VMEM((ts, d_inner), BF16),
        ],
        compiler_params=_cparams(("arbitrary",)),
        name="ssd",
    )(x2d, x2d, shift, scale, shift, scale, n1w, w_zx, w_dt, conv_w, conv_b, dt_bias, a_neg, d_skip_x, norm_w,
      w_out, expand)


def _conf_kernel(x_ref, xn_ref, sh1_ref, sc1_ref, shn_ref, scn_ref, n1w_ref, wgg_ref, ya_ref, g1_ref, sh2_ref,
                 sc2_ref, dw_ref, db_ref, lnw_ref, lnb_ref, wc_ref, wo_ref, n2w_ref, wr_ref, br_ref,
                 x1_ref, h2_ref, lg_ref, pcur_scr, pnxt_scr, hn_scr, ext_scr, rot_scr, cv_scr, *, per_b):
    ts, d = x_ref.shape
    kw = dw_ref.shape[0]
    sub = 8
    rl = rot_scr.shape[1]
    i = pl.program_id(0)

    @pl.when(i == 0)
    def _():
        hn_scr[...] = _norm_mod(x_ref[...], n1w_ref[...], sc1_ref[0], sh1_ref[0])
        _project(hn_scr, wgg_ref, pcur_scr, PROJ_COLS)

    hn_scr[...] = _norm_mod(xn_ref[...], n1w_ref[...], scn_ref[0], shn_ref[0])
    pieces = [functools.partial(_project_piece, hn_scr, wgg_ref, pnxt_scr, j, PROJ_COLS)
              for j in range(wgg_ref.shape[1] // PROJ_COLS)]

    @pl.when(i % per_b == 0)
    def _():
        ext_scr[0:CONF_HALO, :] = jnp.zeros((CONF_HALO, d), F32)

    u_val = pcur_scr[:, 0:d].astype(F32)
    u_gate = pcur_scr[:, d:2 * d].astype(F32)
    ext_scr[CONF_HALO:CONF_HALO + ts, :] = u_val * _sigmoid(u_gate)
    for r in range(1, sub):
        rot_scr[r - 1] = ext_scr[r:r + rl, :]
    cc = LANES
    for j in range(d // cc):
        if pieces:
            pieces.pop(0)()
        cs = slice(j * cc, (j + 1) * cc)
        acc = jnp.broadcast_to(db_ref[:, cs], (ts, cc))
        for k in range(kw):
            off = CONF_HALO - (kw - 1) + k
            r = off % sub
            base = off - r
            tap = ext_scr[base:base + ts, cs] if r == 0 else rot_scr[r - 1, base:base + ts, cs]
            acc = acc + dw_ref[k:k + 1, cs] * tap
        cv_scr[:, cs] = acc
    ext_scr[0:CONF_HALO, :] = ext_scr[ts:ts + CONF_HALO, :]
    while pieces:
        pieces.pop(0)()

    u = cv_scr[...]
    mu = jnp.mean(u, axis=-1, keepdims=True)
    uc = u - mu
    var = jnp.mean(uc * uc, axis=-1, keepdims=True)
    v = _silu(uc * lax.rsqrt(var + EPS) * lnw_ref[...] + lnb_ref[...])
    y_b = _dot(v.astype(BF16), wc_ref[...])
    g_a = pcur_scr[:, 2 * d:3 * d].astype(F32)
    g_b = pcur_scr[:, 3 * d:4 * d].astype(F32)
    merged = _sigmoid(g_a) * ya_ref[...].astype(F32) + _sigmoid(g_b) * y_b
    mix = _dot(merged.astype(BF16), wo_ref[...])
    x1 = x_ref[...] + g1_ref[0] * mix
    x1_ref[...] = x1
    ms = jnp.mean(x1 * x1, axis=-1, keepdims=True)
    h2 = (x1 * lax.rsqrt(ms + EPS) * n2w_ref[...]) * (1.0 + sc2_ref[0]) + sh2_ref[0]
    h2_ref[...] = h2.astype(BF16)
    hh, hl = _split2(h2)
    wh, wl = _split2(wr_ref[...])
    lg_ref[0] = _dot_nt(wh, hh) + _dot_nt(wh, hl) + _dot_nt(wl, hh) + br_ref[...]
    pcur_scr[...] = pnxt_scr[...]


def _conf(x2d, shift1, scale1, n1w, w_gg, y_a, gate1, shift2, scale2, dw_w, dw_b, ln_w, ln_b, w_conf, w_o, n2w,
          wr_t, br, s, ts):
    t, d = x2d.shape
    ne = wr_t.shape[0]
    per_b = s // ts
    n_tiles = t // ts
    assert w_gg.shape[1] == 4 * d
    const = lambda i: (0, 0)
    nxt = lambda i: jnp.minimum(i + 1, n_tiles - 1)
    bvec = lambda f: pl.BlockSpec((1, 1, d), lambda i: (f(i) // per_b, 0, 0))
    cur = lambda i: i
    tile = pl.BlockSpec((ts, d), lambda i: (i, 0))
    return pl.pallas_call(
        functools.partial(_conf_kernel, per_b=per_b),
        grid=(n_tiles,),
        in_specs=[
            tile,
            pl.BlockSpec((ts, d), lambda i: (nxt(i), 0)),
            bvec(cur), bvec(cur), bvec(nxt), bvec(nxt),
            pl.BlockSpec((1, d), const),
            pl.BlockSpec((d, 4 * d), const),
            tile, bvec(cur), bvec(cur), bvec(cur),
            pl.BlockSpec(dw_w.shape, const),
            pl.BlockSpec((1, d), const), pl.BlockSpec((1, d), const), pl.BlockSpec((1, d), const),
            pl.BlockSpec((d, d), const), pl.BlockSpec((d, d), const),
            pl.BlockSpec((1, d), const),
            pl.BlockSpec((ne, d), const), pl.BlockSpec((ne, 1), const),
        ],
        out_specs=[tile, tile, pl.BlockSpec((1, ne, ts), lambda i: (i // per_b, 0, i % per_b))],
        out_shape=[
            jax.ShapeDtypeStruct((t, d), F32),
            jax.ShapeDtypeStruct((t, d), BF16),
            jax.ShapeDtypeStruct((t // s, ne, s), F32),
        ],
        scratch_shapes=[pltpu.VMEM((ts, 4 * d), BF16), pltpu.VMEM((ts, 4 * d), BF16), pltpu.VMEM((ts, d), BF16),
                        pltpu.VMEM((ts + CONF_HALO, d), F32), pltpu.VMEM((7, ts + CONF_HALO - 8, d), F32),
                        pltpu.VMEM((ts, d), F32)],
        compiler_params=_cparams(("arbitrary",)),
        name="conf",
    )(x2d, x2d, shift1, scale1, shift1, scale1, n1w, w_gg, y_a, gate1, shift2, scale2, dw_w, dw_b, ln_w, ln_b,
      w_conf, w_o, n2w, wr_t, br)


def _route_kernel(lg_ref, w_ref, p_ref, cnt_ref):
    ne, tt = lg_ref.shape[1], lg_ref.shape[2]
    cur = lg_ref[0]
    eid = lax.broadcasted_iota(I32, (ne, tt), 0)
    vals, idxs = [], []
    sel = jnp.zeros((ne, tt), F32)
    for _ in range(TOPK):
        m = jnp.max(cur, axis=0, keepdims=True)
        idx = jnp.min(jnp.where(cur == m, eid, ne), axis=0, keepdims=True)
        hit = eid == idx
        sel = jnp.where(hit, 1.0, sel)
        cur = jnp.where(hit, -jnp.inf, cur)
        vals.append(m)
        idxs.append(idx)
    ex = [jnp.exp(v - vals[0]) for v in vals]
    den = ex[0] + ex[1] + ex[2] + ex[3]
    ut = (lax.broadcasted_iota(I32, (tt, tt), 0) <= lax.broadcasted_iota(I32, (tt, tt), 1)).astype(BF16)
    cum = _dot(sel.astype(BF16), ut)
    tot = jnp.broadcast_to(cum[:, tt - 1:tt], (ne, LANES))
    below = (lax.broadcasted_iota(I32, (ne, ne), 0) > lax.broadcasted_iota(I32, (ne, ne), 1)).astype(BF16)
    first = _dot(below, tot.astype(BF16))[:, 0:1]
    slot = first + cum - sel
    for k in range(TOPK):
        w_ref[0, k:k + 1, :] = ex[k] / den
        pk = jnp.sum(jnp.where(eid == idxs[k], slot, 0.0), axis=0, keepdims=True)
        p_ref[0, k:k + 1, :] = pk.astype(I32)
    cnt_ref[0, 0] = tot


def _route(logits_t, tt):
    bsz, ne, s = logits_t.shape
    assert tt <= 256, "tile counts must stay exactly representable in bf16"
    blk = pl.BlockSpec((1, TOPK, tt), lambda b, i: (b, 0, i))
    return pl.pallas_call(
        _route_kernel,
        grid=(bsz, s // tt),
        in_specs=[pl.BlockSpec((1, ne, tt), lambda b, i: (b, 0, i))],
        out_specs=[blk, blk, pl.BlockSpec((1, 1, ne, LANES), lambda b, i: (b, i, 0, 0))],
        out_shape=[
            jax.ShapeDtypeStruct((bsz, TOPK, s), F32),
            jax.ShapeDtypeStruct((bsz, TOPK, s), I32),
            jax.ShapeDtypeStruct((bsz, s // tt, ne, LANES), F32),
        ],
        compiler_params=_cparams(("arbitrary", "arbitrary")),
        name="route",
    )(logits_t)


def _wprep_kernel(w_ref, wg_ref, wl_ref):
    pw = 2 * LANES
    r = lax.broadcasted_iota(I32, (pw, pw), 0)
    c = lax.broadcasted_iota(I32, (pw, pw), 1)
    perm = (r == jnp.where(c < LANES, 2 * c, 2 * (c - LANES) + 1)).astype(BF16)
    for j in range(w_ref.shape[2] // pw):
        blk = _dot(w_ref[0, :, j * pw:(j + 1) * pw].astype(BF16), perm)
        wg_ref[0, :, j * LANES:(j + 1) * LANES] = blk[:, :LANES].astype(BF16)
        wl_ref[0, :, j * LANES:(j + 1) * LANES] = blk[:, LANES:].astype(BF16)


def _wprep(w_gate_up, tm):
    ne, d, n2 = w_gate_up.shape
    out = pl.BlockSpec((1, tm, n2 // 2), lambda e, i: (e, i, 0))
    return pl.pallas_call(
        _wprep_kernel,
        grid=(ne, d // tm),
        in_specs=[pl.BlockSpec((1, tm, n2), lambda e, i: (e, i, 0))],
        out_specs=[out, out],
        out_shape=[jax.ShapeDtypeStruct((ne, d, n2 // 2), BF16)] * 2,
        compiler_params=_cparams(("arbitrary", "arbitrary")),
        name="wprep",
    )(w_gate_up)


SLAB = 8


def _copy_run(src, dst, src_tok, dst_tok, n, sem, nbits):
    done = jnp.int32(0)
    for j in reversed(range(nbits)):
        sz = 1 << j
        s0 = pl.multiple_of((src_tok + done) * SLAB, SLAB)
        d0 = pl.multiple_of((dst_tok + done) * SLAB, SLAB)

        @pl.when((n & sz) != 0)
        def _():
            pltpu.make_async_copy(src.at[pl.ds(s0, sz * SLAB), :], dst.at[pl.ds(d0, sz * SLAB), :], sem).start()

        done = done + (n & sz)


def _dispatch_kernel(rb_ref, rc_ref, re_ref, nu_ref, h_ref, p_ref, xs_hbm, slab, zbuf, sem, zsem, *, rows):
    i = pl.program_id(0)
    last = pl.num_programs(0) - 1
    tt, d = h_ref.shape
    a = tt * TOPK
    ne = re_ref.shape[0]

    def all_runs_landed():
        pltpu.make_async_copy(slab, xs_hbm.at[pl.ds(0, a * SLAB), :], sem).wait()

    @pl.when(i == 0)
    def _():
        zbuf[...] = jnp.zeros(zbuf.shape, F32)
        zero = lambda tok: pltpu.make_async_copy(
            zbuf, xs_hbm.at[pl.ds(pl.multiple_of(tok * SLAB, SLAB), rows * SLAB), :], zsem)
        for e in range(ne):
            zero(re_ref[e]).start()
        for e in range(ne):
            zero(0).wait()
        n_alloc = xs_hbm.shape[0] // (rows * SLAB)
        for j in range(ne + 1):
            @pl.when(nu_ref[0] + j < n_alloc)
            def _():
                zero((nu_ref[0] + j) * rows).start()
        for j in range(ne + 1):
            @pl.when(nu_ref[0] + j < n_alloc)
            def _():
                zero(0).wait()

    pos = p_ref[0]
    rid = lax.broadcasted_iota(I32, (a, tt), 0)
    hit = rid == pos[0:1]
    for k in range(1, TOPK):
        hit = hit | (rid == pos[k:k + 1])
    onehot = hit.astype(BF16)

    @pl.when(i > 0)
    def _():
        all_runs_landed()

    cw = 2 * LANES
    for j in range(d // cw):
        blk = _dot(onehot, h_ref[:, j * cw:(j + 1) * cw])
        slab[pl.ds(2 * j, a, stride=SLAB), :] = blk[:, :LANES]
        slab[pl.ds(2 * j + 1, a, stride=SLAB), :] = blk[:, LANES:]

    def body(e, off):
        n = rc_ref[i * ne + e]
        _copy_run(slab, xs_hbm, off, rb_ref[i * ne + e], n, sem, tt.bit_length())
        return off + n

    lax.fori_loop(0, ne, body, jnp.int32(0))

    @pl.when(i == last)
    def _():
        all_runs_landed()


def _dispatch(run_base, run_cnt, real_end, n_used, h2, pos_t, n_rows, rows, tt):
    t, d = h2.shape
    bsz, _, s = pos_t.shape
    per_b = s // tt
    assert d == SLAB * LANES
    grid_spec = pltpu.PrefetchScalarGridSpec(
        num_scalar_prefetch=4,
        grid=(t // tt,),
        in_specs=[
            pl.BlockSpec((tt, d), lambda i, rb, rc, re, nu: (i, 0)),
            pl.BlockSpec((1, TOPK, tt), lambda i, rb, rc, re, nu: (i // per_b, 0, i % per_b)),
        ],
        out_specs=pl.BlockSpec(memory_space=pl.ANY),
        scratch_shapes=[
            pltpu.VMEM((tt * TOPK * SLAB, LANES), F32),
            pltpu.VMEM((rows * SLAB, LANES), F32),
            pltpu.SemaphoreType.DMA(()),
            pltpu.SemaphoreType.DMA(()),
        ],
    )
    return pl.pallas_call(
        functools.partial(_dispatch_kernel, rows=rows),
        grid_spec=grid_spec,
        out_shape=jax.ShapeDtypeStruct((n_rows * SLAB, LANES), F32),
        compiler_params=_cparams(("arbitrary",)),
        name="dispatch",
    )(run_base, run_cnt, real_end, n_used, h2, pos_t)


def _experts_kernel(be_ref, nu_ref, x_ref, wg_ref, wl_ref, bg_ref, bl_ref, wd_ref, bd_ref, o_ref):
    i = pl.program_id(0)
    rows = x_ref.shape[0] // SLAB

    @pl.when(i < nu_ref[0])
    def _():
        xbf = jnp.concatenate([x_ref[pl.ds(j, rows, stride=SLAB), :] for j in range(SLAB)], axis=1).astype(BF16)
        glu = jnp.minimum(_dot(xbf, wg_ref[0]) + bg_ref[0], SWIGLU_LIMIT)
        lin = jnp.clip(_dot(xbf, wl_ref[0]) + bl_ref[0], -SWIGLU_LIMIT, SWIGLU_LIMIT)
        hid = glu * _sigmoid(SWIGLU_ALPHA * glu) * (lin + 1.0)
        out = _dot(hid.astype(BF16), wd_ref[0]) + bd_ref[0]
        for j in range(SLAB):
            o_ref[pl.ds(j, rows, stride=SLAB), :] = out[:, j * LANES:(j + 1) * LANES]

    @pl.when(i >= nu_ref[0])
    def _():
        o_ref[...] = jnp.zeros(o_ref.shape, F32)


def _experts(block_e, n_used, x_sorted, wg, wl, bg, bl, wd, bd, rows):
    n_blocks = block_e.shape[0]
    d, dff = wg.shape[1], wg.shape[2]
    wspec = lambda shp: pl.BlockSpec((1,) + shp, lambda i, be, nu: (be[i], 0, 0))
    used = lambda i, be, nu: (jnp.minimum(i, nu[0] - 1), 0)
    grid_spec = pltpu.PrefetchScalarGridSpec(
        num_scalar_prefetch=2,
        grid=(n_blocks,),
        in_specs=[
            pl.BlockSpec((rows * SLAB, LANES), used),
            wspec((d, dff)), wspec((d, dff)), wspec((1, dff)), wspec((1, dff)),
            wspec((dff, d)), wspec((1, d)),
        ],
        out_specs=pl.BlockSpec((rows * SLAB, LANES), lambda i, be, nu: (i, 0)),
    )
    return pl.pallas_call(
        _experts_kernel,
        grid_spec=grid_spec,
        out_shape=jax.ShapeDtypeStruct((n_blocks * rows * SLAB, LANES), F32),
        compiler_params=_cparams(("arbitrary",)),
        name="experts",
    )(block_e, n_used, x_sorted, wg, wl, bg, bl, wd, bd)


def _combine_kernel(rb_ref, rc_ref, outs_hbm, p_ref, w_ref, x1_ref, g2_ref, fnw_ref, o_ref, stage, sems):
    i = pl.program_id(0)
    n_tiles = pl.num_programs(0)
    tt, d = x1_ref.shape
    a = tt * TOPK
    ne = rc_ref.shape[0] // n_tiles

    def fetch(tile, slot):
        def body(e, off):
            n = rc_ref[tile * ne + e]
            _copy_run(outs_hbm, stage.at[slot], rb_ref[tile * ne + e], off, n, sems.at[slot], tt.bit_length())
            return off + n

        lax.fori_loop(0, ne, body, jnp.int32(0))

    slot = i % 2

    @pl.when(i == 0)
    def _():
        fetch(i, slot)

    @pl.when(i + 1 < n_tiles)
    def _():
        fetch(i + 1, 1 - slot)

    pltpu.make_async_copy(outs_hbm.at[pl.ds(0, a * SLAB), :], stage.at[slot], sems.at[slot]).wait()

    pos = p_ref[...]
    w = w_ref[...]
    cid = lax.broadcasted_iota(I32, (tt, a), 1)
    pw = jnp.where(cid == pos[:, 0:1], w[:, 0:1], 0.0)
    for k in range(1, TOPK):
        pw = pw + jnp.where(cid == pos[:, k:k + 1], w[:, k:k + 1], 0.0)
    pw_hi, pw_lo = _split2(pw)
    ys = []
    for j in range(0, SLAB, 2):
        so = jnp.concatenate([stage[slot, pl.ds(j, a, stride=SLAB), :],
                              stage[slot, pl.ds(j + 1, a, stride=SLAB), :]], axis=1).astype(BF16)
        ys.append(_dot(pw_hi, so) + _dot(pw_lo, so))
    y = jnp.concatenate(ys, axis=1)
    x2 = x1_ref[...] + g2_ref[0] * y
    ms = jnp.mean(x2 * x2, axis=-1, keepdims=True)
    o_ref[...] = x2 * lax.rsqrt(ms + EPS) * fnw_ref[...]


def _combine(run_base, run_cnt, outs, pos_tm, w_tm, x1, gate2, fnw, s, tt):
    t, d = x1.shape
    per_b = s // tt
    grid_spec = pltpu.PrefetchScalarGridSpec(
        num_scalar_prefetch=2,
        grid=(t // tt,),
        in_specs=[
            pl.BlockSpec(memory_space=pl.ANY),
            pl.BlockSpec((tt, TOPK), lambda i, rb, rc: (i, 0)),
            pl.BlockSpec((tt, TOPK), lambda i, rb, rc: (i, 0)),
            pl.BlockSpec((tt, d), lambda i, rb, rc: (i, 0)),
            pl.BlockSpec((1, 1, d), lambda i, rb, rc: (i // per_b, 0, 0)),
            pl.BlockSpec((1, d), lambda i, rb, rc: (0, 0)),
        ],
        out_specs=pl.BlockSpec((tt, d), lambda i, rb, rc: (i, 0)),
        scratch_shapes=[
            pltpu.VMEM((2, tt * TOPK * SLAB, LANES), F32),
            pltpu.SemaphoreType.DMA((2,)),
        ],
    )
    return pl.pallas_call(
        _combine_kernel,
        grid_spec=grid_spec,
        out_shape=jax.ShapeDtypeStruct((t, d), F32),
        compiler_params=_cparams(("arbitrary",)),
        name="combine",
    )(run_base, run_cnt, outs, pos_tm, w_tm, x1, gate2, fnw)


def _pick(n, pref):
    return pref if n % pref == 0 else n


def kernel(x, c, w_ada, b_ada, norm1_w, w_in, conv_ssm_w, conv_ssm_b, dt_bias, a_log, d_skip, ssm_norm_w, w_ssm_out, conf_dw_w, conf_dw_b, conf_ln_w, conf_ln_b, w_conf_out, w_o, norm2_w, w_router, b_router, w_gate_up, b_gate_up, w_down, b_down, final_norm_w):
    bsz, s, d = x.shape
    depth = w_ada.shape[0]
    d_inner = w_ssm_out.shape[1]
    heads = dt_bias.shape[1]
    conv_dim = conv_ssm_w.shape[2]
    ne = w_router.shape[2]
    t = bsz * s
    assert s % CHUNK == 0 and heads * HEADDIM == d_inner and heads <= LANES
    assert conv_dim == d_inner + 2 * NGROUPS * DSTATE
    assert depth == 1, "the final norm is fused into the (single) layer's combine"

    expand = (jnp.arange(LANES)[:, None] == (jnp.arange(d_inner) // HEADDIM)[None, :]).astype(BF16)
    pad_h = lambda v: jnp.pad(v.astype(F32), (0, LANES - heads)).reshape(1, LANES)

    l = 0
    mod = _ada(c, w_ada[l], b_ada[l])
    shift1, scale1, gate1, shift2, scale2, gate2 = [m.reshape(bsz, 1, d) for m in jnp.split(mod, 6, axis=-1)]

    c0, c1 = d_inner + conv_dim, d_inner + conv_dim + heads
    w_l = w_in[l]
    w_zx = w_l[:, :c0].astype(BF16)
    w_dt = jnp.pad(w_l[:, c0:c1], ((0, 0), (0, LANES - heads))).astype(BF16)
    w_gg = w_l[:, c1:].astype(BF16)
    x2d = x.reshape(t, d)
    n1w = norm1_w[l].reshape(1, d)
    ts = _pick(s, 256)

    y_a = _ssd(x2d, shift1, scale1, n1w, w_zx, w_dt, conv_ssm_w[l], conv_ssm_b[l].reshape(1, conv_dim),
               pad_h(dt_bias[l]), pad_h(-jnp.exp(a_log[l].astype(F32))),
               jnp.repeat(d_skip[l], HEADDIM).reshape(1, d_inner), ssm_norm_w[l].reshape(1, d_inner),
               w_ssm_out[l].astype(BF16), expand, d_inner, s, ts)

    x1, h2, logits_t = _conf(
        x2d, shift1, scale1, n1w, w_gg, y_a, gate1, shift2, scale2, conf_dw_w[l], conf_dw_b[l].reshape(1, d),
        conf_ln_w[l].reshape(1, d), conf_ln_b[l].reshape(1, d), w_conf_out[l].astype(BF16),
        w_o[l].astype(BF16), norm2_w[l].reshape(1, d), w_router[l].T, b_router[l].reshape(ne, 1), s, ts)

    tt = _pick(s, MOE_TILE)
    n_tiles = t // tt
    w_t, pos_t, tcnt = _route(logits_t, tt)

    rows = MOE_ROWS
    n_blocks = (t * TOPK) // rows + ne
    tcnt = tcnt[..., 0].reshape(n_tiles, ne).astype(I32)
    counts = jnp.sum(tcnt, axis=0)
    padded = ((counts + rows - 1) // rows) * rows
    pad_end = jnp.cumsum(padded)
    pad_start = pad_end - padded
    run_base = (pad_start[None, :] + jnp.cumsum(tcnt, axis=0) - tcnt).reshape(-1)
    run_cnt = tcnt.reshape(-1)
    real_end = pad_start + counts
    blk_start = jnp.arange(n_blocks, dtype=I32) * rows
    block_e = jnp.minimum(jnp.sum((pad_end[None, :] <= blk_start[:, None]).astype(I32), axis=1), ne - 1)
    n_used = (pad_end[ne - 1:ne] // rows).astype(I32)

    x_sorted = _dispatch(run_base, run_cnt, real_end, n_used, h2, pos_t, (n_blocks + 1) * rows, rows, tt)

    wg, wl = _wprep(w_gate_up[l], _pick(d, 512))
    bgu = b_gate_up[l]
    dff = wg.shape[2]
    outs = _experts(block_e, n_used, x_sorted, wg, wl,
                    bgu[:, 0::2].reshape(ne, 1, dff), bgu[:, 1::2].reshape(ne, 1, dff),
                    w_down[l].astype(BF16), b_down[l].reshape(ne, 1, d), rows)

    pos_tm = jnp.transpose(pos_t, (0, 2, 1)).reshape(t, TOPK)
    w_tm = jnp.transpose(w_t, (0, 2, 1)).reshape(t, TOPK)
    out = _combine(run_base, run_cnt, outs, pos_tm, w_tm, x1, gate2, final_norm_w.reshape(1, d), s, tt)
    return out.reshape(bsz, s, d)
```

```python
import functools

import jax
import jax.numpy as jnp
from jax import lax
from jax.experimental import pallas as pl
from jax.experimental.pallas import tpu as pltpu

F32 = jnp.float32
BF16 = jnp.bfloat16
I32 = jnp.int32

EPS = 1e-6
HEADDIM = 64
NGROUPS = 8
DSTATE = 128
CHUNK = 128
CONV_HALO = 8
CONF_HALO = 32
TOPK = 4
SWIGLU_LIMIT = 7.0
SWIGLU_ALPHA = 1.702
LANES = 128
MOE_ROWS = 512
MOE_TILE = 256
PROJ_COLS = 512
VMEM_LIMIT = 56 * 1024 * 1024


def _cparams(sem):
    return pltpu.CompilerParams(dimension_semantics=sem, vmem_limit_bytes=VMEM_LIMIT)


def _sigmoid(v):
    return 0.5 * jnp.tanh(0.5 * v) + 0.5


def _silu(v):
    return v * _sigmoid(v)


def _split2(a):
    hi = a.astype(BF16)
    lo = (a - hi.astype(F32)).astype(BF16)
    return hi, lo


def _split3(a):
    hi = a.astype(BF16)
    r = a - hi.astype(F32)
    mid = r.astype(BF16)
    lo = (r - mid.astype(F32)).astype(BF16)
    return hi, mid, lo


def _dot(a, b):
    return jnp.dot(a, b, preferred_element_type=F32)


def _dot_nt(a, b):
    return lax.dot_general(a, b, (((1,), (1,)), ((), ())), preferred_element_type=F32)


def _dot_tn(a, b):
    return lax.dot_general(a, b, (((0,), (0,)), ((), ())), preferred_element_type=F32)


def _dot_f32(a, b):
    ah, al = _split2(a)
    bh, bl = _split2(b)
    return _dot(ah, bh) + _dot(ah, bl) + _dot(al, bh)


def _ada_kernel(c_ref, w_ref, b_ref, o_ref):
    c = c_ref[...]
    o_ref[...] = _dot_f32(_silu(c), w_ref[...]) + b_ref[...]


def _ada(c, w_ada, b_ada):
    bsz, d = c.shape
    n = w_ada.shape[1]
    tn = d
    return pl.pallas_call(
        _ada_kernel,
        grid=(n // tn,),
        in_specs=[
            pl.BlockSpec((bsz, d), lambda j: (0, 0)),
            pl.BlockSpec((d, tn), lambda j: (0, j)),
            pl.BlockSpec((1, tn), lambda j: (0, j)),
        ],
        out_specs=pl.BlockSpec((bsz, tn), lambda j: (0, j)),
        out_shape=jax.ShapeDtypeStruct((bsz, n), F32),
        compiler_params=_cparams(("arbitrary",)),
        name="ada",
    )(c, w_ada, b_ada.reshape(1, n))


def _norm_mod(x, nw, sc, sh):
    ms = jnp.mean(x * x, axis=-1, keepdims=True)
    return ((x * lax.rsqrt(ms + EPS) * nw) * (1.0 + sc) + sh).astype(BF16)


def _project_piece(h_ref, w_ref, dst_ref, j, cw):
    dst_ref[:, j * cw:(j + 1) * cw] = _dot(h_ref[...], w_ref[:, j * cw:(j + 1) * cw]).astype(BF16)


def _project(h_ref, w_ref, dst_ref, cw):
    for j in range(w_ref.shape[1] // cw):
        _project_piece(h_ref, w_ref, dst_ref, j, cw)


def _ssd_kernel(x0_ref, xn_ref, sh0_ref, sc0_ref, shn_ref, scn_ref, n1w_ref, wzx_ref, wdt_ref,
                cw_ref, cb_ref, dtb_ref, a_ref, dsk_ref, nw_ref, wout_ref, e_ref, o_ref,
                pcur_scr, pnxt_scr, dcur_scr, dnxt_scr, hn_scr, ext_scr, xc_scr, state_scr, yn_scr, ex_scr, *, per_b):
    L = CHUNK
    ts = xn_ref.shape[0]
    d_inner = yn_scr.shape[1]
    gw = d_inner // NGROUPS
    hpg = gw // HEADDIM
    conv_w = cw_ref.shape[0]
    i = pl.program_id(0)

    @pl.when(i == 0)
    def _():
        hn_scr[...] = _norm_mod(x0_ref[...], n1w_ref[...], sc0_ref[0], sh0_ref[0])
        _project(hn_scr, wzx_ref, pcur_scr, PROJ_COLS)
        dcur_scr[...] = _dot(hn_scr[...], wdt_ref[...])

    hn_scr[...] = _norm_mod(xn_ref[...], n1w_ref[...], scn_ref[0], shn_ref[0])
    pieces = [functools.partial(_project_piece, hn_scr, wzx_ref, pnxt_scr, j, PROJ_COLS)
              for j in range(wzx_ref.shape[1] // PROJ_COLS)]

    def dt_piece():
        dnxt_scr[...] = _dot(hn_scr[...], wdt_ref[...])

    pieces.append(dt_piece)

    @pl.when(i % per_b == 0)
    def _():
        ext_scr[0:CONV_HALO, :] = jnp.zeros((CONV_HALO, ext_scr.shape[1]), F32)
        state_scr[...] = jnp.zeros(state_scr.shape, F32)

    row = lax.broadcasted_iota(I32, (L, L), 0)
    col = lax.broadcasted_iota(I32, (L, L), 1)
    causal = row >= col
    tri = causal.astype(BF16)
    head_blk = (lax.broadcasted_iota(I32, (hpg * L, gw), 0) // L) == (lax.broadcasted_iota(I32, (hpg * L, gw), 1) // HEADDIM)

    for c in range(ts // L):
        rs = slice(c * L, (c + 1) * L)
        ext_scr[CONV_HALO:CONV_HALO + L, :] = pcur_scr[rs, d_inner:].astype(F32)
        cc = 256
        for j in range(ext_scr.shape[1] // cc):
            cs = slice(j * cc, (j + 1) * cc)
            acc = jnp.broadcast_to(cb_ref[:, cs], (L, cc))
            for k in range(conv_w):
                off = CONV_HALO - (conv_w - 1) + k
                acc = acc + cw_ref[k:k + 1, cs] * ext_scr[off:off + L, cs]
            xc_scr[:, cs] = _silu(acc)
        ext_scr[0:CONV_HALO, :] = ext_scr[L:L + CONV_HALO, :]

        raw = dcur_scr[rs, :] + dtb_ref[...]
        dt = jnp.maximum(raw, 0.0) + jnp.log(1.0 + jnp.exp(-jnp.abs(raw)))
        adt = dt * a_ref[...]
        a_hi, a_mid, a_lo = _split3(adt)
        acs = _dot(tri, a_hi) + _dot(tri, a_mid) + _dot(tri, a_lo)
        acs_t = acs.T
        last = acs[L - 1:L, :]
        stack = jnp.concatenate(
            [dt, jnp.exp(acs), jnp.exp(last - acs), jnp.broadcast_to(jnp.exp(last), (8, LANES))], axis=0)
        s_hi, s_lo = _split2(stack)
        ex_scr[...] = _dot(s_hi, e_ref[...]) + _dot(s_lo, e_ref[...])

        for g in range(NGROUPS):
            if pieces:
                pieces.pop(0)()
            gs = slice(g * gw, (g + 1) * gw)
            dt_x, ea_x, dte_x, cd_x = (ex_scr[0:L, gs], ex_scr[L:2 * L, gs], ex_scr[2 * L:3 * L, gs],
                                       ex_scr[3 * L:3 * L + 1, gs])
            xs_g = xc_scr[:, gs]
            b_g = xc_scr[:, d_inner + g * DSTATE:d_inner + (g + 1) * DSTATE].astype(BF16)
            c_g = xc_scr[:, d_inner + NGROUPS * DSTATE + g * DSTATE:d_inner + NGROUPS * DSTATE + (g + 1) * DSTATE].astype(BF16)
            xdt = xs_g * dt_x
            xdt_b = xdt.astype(BF16)
            cb = _dot_nt(c_g, b_g)
            ms_ = []
            for hh in range(hpg):
                h = g * hpg + hh
                seg = acs[:, h:h + 1] - acs_t[h:h + 1, :]
                dec = jnp.where(causal, jnp.exp(seg), 0.0)
                ms_.append((cb * dec).astype(BF16))
            y = _dot(jnp.concatenate(ms_, axis=1), jnp.where(head_blk, jnp.concatenate([xdt_b] * hpg, axis=0), 0))
            st = state_scr[g]
            y = y + _dot(c_g, st.astype(BF16)) * ea_x
            state_scr[g] = st * cd_x + _dot_tn(b_g, (xdt * dte_x).astype(BF16))
            y = y + dsk_ref[:, gs] * xs_g
            zg = pcur_scr[rs, gs].astype(F32)
            y = y * _silu(zg)
            ms = jnp.mean(y * y, axis=-1, keepdims=True)
            yn_scr[rs, gs] = (y * lax.rsqrt(ms + EPS) * nw_ref[:, gs]).astype(BF16)

    while pieces:
        pieces.pop(0)()
    o_ref[...] = _dot(yn_scr[...], wout_ref[...]).astype(BF16)
    pcur_scr[...] = pnxt_scr[...]
    dcur_scr[...] = dnxt_scr[...]


def _ssd(x2d, shift, scale, n1w, w_zx, w_dt, conv_w, conv_b, dt_bias, a_neg, d_skip_x, norm_w, w_out, expand,
         d_inner, s, ts):
    t, d = x2d.shape
    conv_dim = conv_w.shape[1]
    n_zx = w_zx.shape[1]
    assert n_zx == d_inner + conv_dim
    per_b = s // ts
    n_tiles = t // ts
    const = lambda i: (0, 0)
    nxt = lambda i: jnp.minimum(i + 1, n_tiles - 1)
    bvec = lambda f: pl.BlockSpec((1, 1, d), lambda i: (f(i) // per_b, 0, 0))
    return pl.pallas_call(
        functools.partial(_ssd_kernel, per_b=per_b),
        grid=(n_tiles,),
        in_specs=[
            pl.BlockSpec((ts, d), const),
            pl.BlockSpec((ts, d), lambda i: (nxt(i), 0)),
            bvec(lambda i: 0 * i), bvec(lambda i: 0 * i), bvec(nxt), bvec(nxt),
            pl.BlockSpec((1, d), const),
            pl.BlockSpec((d, n_zx), const),
            pl.BlockSpec((d, LANES), const),
            pl.BlockSpec(conv_w.shape, const),
            pl.BlockSpec((1, conv_dim), const),
            pl.BlockSpec((1, LANES), const),
            pl.BlockSpec((1, LANES), const),
            pl.BlockSpec((1, d_inner), const),
            pl.BlockSpec((1, d_inner), const),
            pl.BlockSpec((d_inner, d), const),
            pl.BlockSpec((LANES, d_inner), const),
        ],
        out_specs=pl.BlockSpec((ts, d), lambda i: (i, 0)),
        out_shape=jax.ShapeDtypeStruct((t, d), BF16),
        scratch_shapes=[
            pltpu.VMEM((ts, n_zx), BF16), pltpu.VMEM((ts, n_zx), BF16),
            pltpu.VMEM((ts, LANES), F32), pltpu.VMEM((ts, LANES), F32),
            pltpu.VMEM((ts, d), BF16),
            pltpu.VMEM((CHUNK + CONV_HALO, conv_dim), F32),
            pltpu.VMEM((CHUNK, conv_dim), F32),
            pltpu.VMEM((NGROUPS, DSTATE, d_inner // NGROUPS), F32),
            pltpu.VMEM((ts, d_inner), BF16),
            pltpu.VMEM((3 * CHUNK + 8, d_inner), F32),
        ],
        compiler_params=_cparams(("arbitrary",)),
        name="ssd",
    )(x2d, x2d, shift, scale, shift, scale, n1w, w_zx, w_dt, conv_w, conv_b, dt_bias, a_neg, d_skip_x, norm_w,
      w_out, expand)


def _conf_kernel(x_ref, xn_ref, sh1_ref, sc1_ref, shn_ref, scn_ref, n1w_ref, wgg_ref, ya_ref, g1_ref, sh2_ref,
                 sc2_ref, dw_ref, db_ref, lnw_ref, lnb_ref, wc_ref, wo_ref, n2w_ref, wr_ref, br_ref,
                 x1_ref, h2_ref, lg_ref, pcur_scr, pnxt_scr, hn_scr, ext_scr, rot_scr, cv_scr, *, per_b):
    ts, d = x_ref.shape
    kw = dw_ref.shape[0]
    sub = 8
    rl = rot_scr.shape[1]
    i = pl.program_id(0)

    @pl.when(i == 0)
    def _():
        hn_scr[...] = _norm_mod(x_ref[...], n1w_ref[...], sc1_ref[0], sh1_ref[0])
        _project(hn_scr, wgg_ref, pcur_scr, PROJ_COLS)

    hn_scr[...] = _norm_mod(xn_ref[...], n1w_ref[...], scn_ref[0], shn_ref[0])
    pieces = [functools.partial(_project_piece, hn_scr, wgg_ref, pnxt_scr, j, PROJ_COLS)
              for j in range(wgg_ref.shape[1] // PROJ_COLS)]

    @pl.when(i % per_b == 0)
    def _():
        ext_scr[0:CONF_HALO, :] = jnp.zeros((CONF_HALO, d), F32)

    u_val = pcur_scr[:, 0:d].astype(F32)
    u_gate = pcur_scr[:, d:2 * d].astype(F32)
    ext_scr[CONF_HALO:CONF_HALO + ts, :] = u_val * _sigmoid(u_gate)
    for r in range(1, sub):
        rot_scr[r - 1] = ext_scr[r:r + rl, :]
    cc = LANES
    for j in range(d // cc):
        if pieces:
            pieces.pop(0)()
        cs = slice(j * cc, (j + 1) * cc)
        acc = jnp.broadcast_to(db_ref[:, cs], (ts, cc))
        for k in range(kw):
            off = CONF_HALO - (kw - 1) + k
            r = off % sub
            base = off - r
            tap = ext_scr[base:base + ts, cs] if r == 0 else rot_scr[r - 1, base:base + ts, cs]
            acc = acc + dw_ref[k:k + 1, cs] * tap
        cv_scr[:, cs] = acc
    ext_scr[0:CONF_HALO, :] = ext_scr[ts:ts + CONF_HALO, :]
    while pieces:
        pieces.pop(0)()

    u = cv_scr[...]
    mu = jnp.mean(u, axis=-1, keepdims=True)
    uc = u - mu
    var = jnp.mean(uc * uc, axis=-1, keepdims=True)
    v = _silu(uc * lax.rsqrt(var + EPS) * lnw_ref[...] + lnb_ref[...])
    y_b = _dot(v.astype(BF16), wc_ref[...])
    g_a = pcur_scr[:, 2 * d:3 * d].astype(F32)
    g_b = pcur_scr[:, 3 * d:4 * d].astype(F32)
    merged = _sigmoid(g_a) * ya_ref[...].astype(F32) + _sigmoid(g_b) * y_b
    mix = _dot(merged.astype(BF16), wo_ref[...])
    x1 = x_ref[...] + g1_ref[0] * mix
    x1_ref[...] = x1
    ms = jnp.mean(x1 * x1, axis=-1, keepdims=True)
    h2 = (x1 * lax.rsqrt(ms + EPS) * n2w_ref[...]) * (1.0 + sc2_ref[0]) + sh2_ref[0]
    h2_ref[...] = h2.astype(BF16)
    hh, hl = _split2(h2)
    wh, wl = _split2(wr_ref[...])
    lg_ref[0] = _dot_nt(wh, hh) + _dot_nt(wh, hl) + _dot_nt(wl, hh) + br_ref[...]
    pcur_scr[...] = pnxt_scr[...]


def _conf(x2d, shift1, scale1, n1w, w_gg, y_a, gate1, shift2, scale2, dw_w, dw_b, ln_w, ln_b, w_conf, w_o, n2w,
          wr_t, br, s, ts):
    t, d = x2d.shape
    ne = wr_t.shape[0]
    per_b = s // ts
    n_tiles = t // ts
    assert w_gg.shape[1] == 4 * d
    const = lambda i: (0, 0)
    nxt = lambda i: jnp.minimum(i + 1, n_tiles - 1)
    bvec = lambda f: pl.BlockSpec((1, 1, d), lambda i: (f(i) // per_b, 0, 0))
    cur = lambda i: i
    tile = pl.BlockSpec((ts, d), lambda i: (i, 0))
    return pl.pallas_call(
        functools.partial(_conf_kernel, per_b=per_b),
        grid=(n_tiles,),
        in_specs=[
            tile,
            pl.BlockSpec((ts, d), lambda i: (nxt(i), 0)),
            bvec(cur), bvec(cur), bvec(nxt), bvec(nxt),
            pl.BlockSpec((1, d), const),
            pl.BlockSpec((d, 4 * d), const),
            tile, bvec(cur), bvec(cur), bvec(cur),
            pl.BlockSpec(dw_w.shape, const),
            pl.BlockSpec((1, d), const), pl.BlockSpec((1, d), const), pl.BlockSpec((1, d), const),
            pl.BlockSpec((d, d), const), pl.BlockSpec((d, d), const),
            pl.BlockSpec((1, d), const),
            pl.BlockSpec((ne, d), const), pl.BlockSpec((ne, 1), const),
        ],
        out_specs=[tile, tile, pl.BlockSpec((1, ne, ts), lambda i: (i // per_b, 0, i % per_b))],
        out_shape=[
            jax.ShapeDtypeStruct((t, d), F32),
            jax.ShapeDtypeStruct((t, d), BF16),
            jax.ShapeDtypeStruct((t // s, ne, s), F32),
        ],
        scratch_shapes=[pltpu.VMEM((ts, 4 * d), BF16), pltpu.VMEM((ts, 4 * d), BF16), pltpu.VMEM((ts, d), BF16),
                        pltpu.VMEM((ts + CONF_HALO, d), F32), pltpu.VMEM((7, ts + CONF_HALO - 8, d), F32),
                        pltpu.VMEM((ts, d), F32)],
        compiler_params=_cparams(("arbitrary",)),
        name="conf",
    )(x2d, x2d, shift1, scale1, shift1, scale1, n1w, w_gg, y_a, gate1, shift2, scale2, dw_w, dw_b, ln_w, ln_b,
      w_conf, w_o, n2w, wr_t, br)


def _route_kernel(lg_ref, w_ref, p_ref, cnt_ref):
    ne, tt = lg_ref.shape[1], lg_ref.shape[2]
    cur = lg_ref[0]
    eid = lax.broadcasted_iota(I32, (ne, tt), 0)
    vals, idxs = [], []
    sel = jnp.zeros((ne, tt), F32)
    for _ in range(TOPK):
        m = jnp.max(cur, axis=0, keepdims=True)
        idx = jnp.min(jnp.where(cur == m, eid, ne), axis=0, keepdims=True)
        hit = eid == idx
        sel = jnp.where(hit, 1.0, sel)
        cur = jnp.where(hit, -jnp.inf, cur)
        vals.append(m)
        idxs.append(idx)
    ex = [jnp.exp(v - vals[0]) for v in vals]
    den = ex[0] + ex[1] + ex[2] + ex[3]
    ut = (lax.broadcasted_iota(I32, (tt, tt), 0) <= lax.broadcasted_iota(I32, (tt, tt), 1)).astype(BF16)
    cum = _dot(sel.astype(BF16), ut)
    tot = jnp.broadcast_to(cum[:, tt - 1:tt], (ne, LANES))
    below = (lax.broadcasted_iota(I32, (ne, ne), 0) > lax.broadcasted_iota(I32, (ne, ne), 1)).astype(BF16)
    first = _dot(below, tot.astype(BF16))[:, 0:1]
    slot = first + cum - sel
    for k in range(TOPK):
        w_ref[0, k:k + 1, :] = ex[k] / den
        pk = jnp.sum(jnp.where(eid == idxs[k], slot, 0.0), axis=0, keepdims=True)
        p_ref[0, k:k + 1, :] = pk.astype(I32)
    cnt_ref[0, 0] = tot


def _route(logits_t, tt):
    bsz, ne, s = logits_t.shape
    assert tt <= 256, "tile counts must stay exactly representable in bf16"
    blk = pl.BlockSpec((1, TOPK, tt), lambda b, i: (b, 0, i))
    return pl.pallas_call(
        _route_kernel,
        grid=(bsz, s // tt),
        in_specs=[pl.BlockSpec((1, ne, tt), lambda b, i: (b, 0, i))],
        out_specs=[blk, blk, pl.BlockSpec((1, 1, ne, LANES), lambda b, i: (b, i, 0, 0))],
        out_shape=[
            jax.ShapeDtypeStruct((bsz, TOPK, s), F32),
            jax.ShapeDtypeStruct((bsz, TOPK, s), I32),
            jax.ShapeDtypeStruct((bsz, s // tt, ne, LANES), F32),
        ],
        compiler_params=_cparams(("arbitrary", "arbitrary")),
        name="route",
    )(logits_t)


def _wprep_kernel(w_ref, wg_ref, wl_ref):
    pw = 2 * LANES
    r = lax.broadcasted_iota(I32, (pw, pw), 0)
    c = lax.broadcasted_iota(I32, (pw, pw), 1)
    perm = (r == jnp.where(c < LANES, 2 * c, 2 * (c - LANES) + 1)).astype(BF16)
    for j in range(w_ref.shape[2] // pw):
        blk = _dot(w_ref[0, :, j * pw:(j + 1) * pw].astype(BF16), perm)
        wg_ref[0, :, j * LANES:(j + 1) * LANES] = blk[:, :LANES].astype(BF16)
        wl_ref[0, :, j * LANES:(j + 1) * LANES] = blk[:, LANES:].astype(BF16)


def _wprep(w_gate_up, tm):
    ne, d, n2 = w_gate_up.shape
    out = pl.BlockSpec((1, tm, n2 // 2), lambda e, i: (e, i, 0))
    return pl.pallas_call(
        _wprep_kernel,
        grid=(ne, d // tm),
        in_specs=[pl.BlockSpec((1, tm, n2), lambda e, i: (e, i, 0))],
        out_specs=[out, out],
        out_shape=[jax.ShapeDtypeStruct((ne, d, n2 // 2), BF16)] * 2,
        compiler_params=_cparams(("arbitrary", "arbitrary")),
        name="wprep",
    )(w_gate_up)


SLAB = 8


def _copy_run(src, dst, src_tok, dst_tok, n, sem, nbits):
    done = jnp.int32(0)
    for j in reversed(range(nbits)):
        sz = 1 << j
        s0 = pl.multiple_of((src_tok + done) * SLAB, SLAB)
        d0 = pl.multiple_of((dst_tok + done) * SLAB, SLAB)

        @pl.when((n & sz) != 0)
        def _():
            pltpu.make_async_copy(src.at[pl.ds(s0, sz * SLAB), :], dst.at[pl.ds(d0, sz * SLAB), :], sem).start()

        done = done + (n & sz)


def _dispatch_kernel(rb_ref, rc_ref, re_ref, nu_ref, h_ref, p_ref, xs_hbm, slab, zbuf, sem, zsem, *, rows):
    i = pl.program_id(0)
    last = pl.num_programs(0) - 1
    tt, d = h_ref.shape
    a = tt * TOPK
    ne = re_ref.shape[0]

    def all_runs_landed():
        pltpu.make_async_copy(slab, xs_hbm.at[pl.ds(0, a * SLAB), :], sem).wait()

    @pl.when(i == 0)
    def _():
        zbuf[...] = jnp.zeros(zbuf.shape, F32)
        zero = lambda tok: pltpu.make_async_copy(
            zbuf, xs_hbm.at[pl.ds(pl.multiple_of(tok * SLAB, SLAB), rows * SLAB), :], zsem)
        for e in range(ne):
            zero(re_ref[e]).start()
        for e in range(ne):
            zero(0).wait()
        n_alloc = xs_hbm.shape[0] // (rows * SLAB)
        for j in range(ne + 1):
            @pl.when(nu_ref[0] + j < n_alloc)
            def _():
                zero((nu_ref[0] + j) * rows).start()
        for j in range(ne + 1):
            @pl.when(nu_ref[0] + j < n_alloc)
            def _():
                zero(0).wait()

    pos = p_ref[0]
    rid = lax.broadcasted_iota(I32, (a, tt), 0)
    hit = rid == pos[0:1]
    for k in range(1, TOPK):
        hit = hit | (rid == pos[k:k + 1])
    onehot = hit.astype(BF16)

    @pl.when(i > 0)
    def _():
        all_runs_landed()

    cw = 2 * LANES
    for j in range(d // cw):
        blk = _dot(onehot, h_ref[:, j * cw:(j + 1) * cw])
        slab[pl.ds(2 * j, a, stride=SLAB), :] = blk[:, :LANES]
        slab[pl.ds(2 * j + 1, a, stride=SLAB), :] = blk[:, LANES:]

    def body(e, off):
        n = rc_ref[i * ne + e]
        _copy_run(slab, xs_hbm, off, rb_ref[i * ne + e], n, sem, tt.bit_length())
        return off + n

    lax.fori_loop(0, ne, body, jnp.int32(0))

    @pl.when(i == last)
    def _():
        all_runs_landed()


def _dispatch(run_base, run_cnt, real_end, n_used, h2, pos_t, n_rows, rows, tt):
    t, d = h2.shape
    bsz, _, s = pos_t.shape
    per_b = s // tt
    assert d == SLAB * LANES
    grid_spec = pltpu.PrefetchScalarGridSpec(
        num_scalar_prefetch=4,
        grid=(t // tt,),
        in_specs=[
            pl.BlockSpec((tt, d), lambda i, rb, rc, re, nu: (i, 0)),
            pl.BlockSpec((1, TOPK, tt), lambda i, rb, rc, re, nu: (i // per_b, 0, i % per_b)),
        ],
        out_specs=pl.BlockSpec(memory_space=pl.ANY),
        scratch_shapes=[
            pltpu.VMEM((tt * TOPK * SLAB, LANES), F32),
            pltpu.VMEM((rows * SLAB, LANES), F32),
            pltpu.SemaphoreType.DMA(()),
            pltpu.SemaphoreType.DMA(()),
        ],
    )
    return pl.pallas_call(
        functools.partial(_dispatch_kernel, rows=rows),
        grid_spec=grid_spec,
        out_shape=jax.ShapeDtypeStruct((n_rows * SLAB, LANES), F32),
        compiler_params=_cparams(("arbitrary",)),
        name="dispatch",
    )(run_base, run_cnt, real_end, n_used, h2, pos_t)


def _experts_kernel(be_ref, nu_ref, x_ref, wg_ref, wl_ref, bg_ref, bl_ref, wd_ref, bd_ref, o_ref):
    i = pl.program_id(0)
    rows = x_ref.shape[0] // SLAB

    @pl.when(i < nu_ref[0])
    def _():
        xbf = jnp.concatenate([x_ref[pl.ds(j, rows, stride=SLAB), :] for j in range(SLAB)], axis=1).astype(BF16)
        glu = jnp.minimum(_dot(xbf, wg_ref[0]) + bg_ref[0], SWIGLU_LIMIT)
        lin = jnp.clip(_dot(xbf, wl_ref[0]) + bl_ref[0], -SWIGLU_LIMIT, SWIGLU_LIMIT)
        hid = glu * _sigmoid(SWIGLU_ALPHA * glu) * (lin + 1.0)
        out = _dot(hid.astype(BF16), wd_ref[0]) + bd_ref[0]
        for j in range(SLAB):
            o_ref[pl.ds(j, rows, stride=SLAB), :] = out[:, j * LANES:(j + 1) * LANES]

    @pl.when(i >= nu_ref[0])
    def _():
        o_ref[...] = jnp.zeros(o_ref.shape, F32)


def _experts(block_e, n_used, x_sorted, wg, wl, bg, bl, wd, bd, rows):
    n_blocks = block_e.shape[0]
    d, dff = wg.shape[1], wg.shape[2]
    wspec = lambda shp: pl.BlockSpec((1,) + shp, lambda i, be, nu: (be[i], 0, 0))
    used = lambda i, be, nu: (jnp.minimum(i, nu[0] - 1), 0)
    grid_spec = pltpu.PrefetchScalarGridSpec(
        num_scalar_prefetch=2,
        grid=(n_blocks,),
        in_specs=[
            pl.BlockSpec((rows * SLAB, LANES), used),
            wspec((d, dff)), wspec((d, dff)), wspec((1, dff)), wspec((1, dff)),
            wspec((dff, d)), wspec((1, d)),
        ],
        out_specs=pl.BlockSpec((rows * SLAB, LANES), lambda i, be, nu: (i, 0)),
    )
    return pl.pallas_call(
        _experts_kernel,
        grid_spec=grid_spec,
        out_shape=jax.ShapeDtypeStruct((n_blocks * rows * SLAB, LANES), F32),
        compiler_params=_cparams(("arbitrary",)),
        name="experts",
    )(block_e, n_used, x_sorted, wg, wl, bg, bl, wd, bd)


def _combine_kernel(rb_ref, rc_ref, outs_hbm, p_ref, w_ref, x1_ref, g2_ref, fnw_ref, o_ref, stage, sems):
    i = pl.program_id(0)
    n_tiles = pl.num_programs(0)
    tt, d = x1_ref.shape
    a = tt * TOPK
    ne = rc_ref.shape[0] // n_tiles

    def fetch(tile, slot):
        def body(e, off):
            n = rc_ref[tile * ne + e]
            _copy_run(outs_hbm, stage.at[slot], rb_ref[tile * ne + e], off, n, sems.at[slot], tt.bit_length())
            return off + n

        lax.fori_loop(0, ne, body, jnp.int32(0))

    slot = i % 2

    @pl.when(i == 0)
    def _():
        fetch(i, slot)

    @pl.when(i + 1 < n_tiles)
    def _():
        fetch(i + 1, 1 - slot)

    pltpu.make_async_copy(outs_hbm.at[pl.ds(0, a * SLAB), :], stage.at[slot], sems.at[slot]).wait()

    pos = p_ref[...]
    w = w_ref[...]
    cid = lax.broadcasted_iota(I32, (tt, a), 1)
    pw = jnp.where(cid == pos[:, 0:1], w[:, 0:1], 0.0)
    for k in range(1, TOPK):
        pw = pw + jnp.where(cid == pos[:, k:k + 1], w[:, k:k + 1], 0.0)
    pw_hi, pw_lo = _split2(pw)
    ys = []
    for j in range(0, SLAB, 2):
        so = jnp.concatenate([stage[slot, pl.ds(j, a, stride=SLAB), :],
                              stage[slot, pl.ds(j + 1, a, stride=SLAB), :]], axis=1).astype(BF16)
        ys.append(_dot(pw_hi, so) + _dot(pw_lo, so))
    y = jnp.concatenate(ys, axis=1)
    x2 = x1_ref[...] + g2_ref[0] * y
    ms = jnp.mean(x2 * x2, axis=-1, keepdims=True)
    o_ref[...] = x2 * lax.rsqrt(ms + EPS) * fnw_ref[...]


def _combine(run_base, run_cnt, outs, pos_tm, w_tm, x1, gate2, fnw, s, tt):
    t, d = x1.shape
    per_b = s // tt
    grid_spec = pltpu.PrefetchScalarGridSpec(
        num_scalar_prefetch=2,
        grid=(t // tt,),
        in_specs=[
            pl.BlockSpec(memory_space=pl.ANY),
            pl.BlockSpec((tt, TOPK), lambda i, rb, rc: (i, 0)),
            pl.BlockSpec((tt, TOPK), lambda i, rb, rc: (i, 0)),
            pl.BlockSpec((tt, d), lambda i, rb, rc: (i, 0)),
            pl.BlockSpec((1, 1, d), lambda i, rb, rc: (i // per_b, 0, 0)),
            pl.BlockSpec((1, d), lambda i, rb, rc: (0, 0)),
        ],
        out_specs=pl.BlockSpec((tt, d), lambda i, rb, rc: (i, 0)),
        scratch_shapes=[
            pltpu.VMEM((2, tt * TOPK * SLAB, LANES), F32),
            pltpu.SemaphoreType.DMA((2,)),
        ],
    )
    return pl.pallas_call(
        _combine_kernel,
        grid_spec=grid_spec,
        out_shape=jax.ShapeDtypeStruct((t, d), F32),
        compiler_params=_cparams(("arbitrary",)),
        name="combine",
    )(run_base, run_cnt, outs, pos_tm, w_tm, x1, gate2, fnw)


def _pick(n, pref):
    return pref if n % pref == 0 else n


def kernel(x, c, w_ada, b_ada, norm1_w, w_in, conv_ssm_w, conv_ssm_b, dt_bias, a_log, d_skip, ssm_norm_w, w_ssm_out, conf_dw_w, conf_dw_b, conf_ln_w, conf_ln_b, w_conf_out, w_o, norm2_w, w_router, b_router, w_gate_up, b_gate_up, w_down, b_down, final_norm_w):
    bsz, s, d = x.shape
    depth = w_ada.shape[0]
    d_inner = w_ssm_out.shape[1]
    heads = dt_bias.shape[1]
    conv_dim = conv_ssm_w.shape[2]
    ne = w_router.shape[2]
    t = bsz * s
    assert s % CHUNK == 0 and heads * HEADDIM == d_inner and heads <= LANES
    assert conv_dim == d_inner + 2 * NGROUPS * DSTATE
    assert depth == 1, "the final norm is fused into the (single) layer's combine"

    expand = (jnp.arange(LANES)[:, None] == (jnp.arange(d_inner) // HEADDIM)[None, :]).astype(BF16)
    pad_h = lambda v: jnp.pad(v.astype(F32), (0, LANES - heads)).reshape(1, LANES)

    l = 0
    mod = _ada(c, w_ada[l], b_ada[l])
    shift1, scale1, gate1, shift2, scale2, gate2 = [m.reshape(bsz, 1, d) for m in jnp.split(mod, 6, axis=-1)]

    c0, c1 = d_inner + conv_dim, d_inner + conv_dim + heads
    w_l = w_in[l]
    w_zx = w_l[:, :c0].astype(BF16)
    w_dt = jnp.pad(w_l[:, c0:c1], ((0, 0), (0, LANES - heads))).astype(BF16)
    w_gg = w_l[:, c1:].astype(BF16)
    x2d = x.reshape(t, d)
    n1w = norm1_w[l].reshape(1, d)
    ts = _pick(s, 256)

    y_a = _ssd(x2d, shift1, scale1, n1w, w_zx, w_dt, conv_ssm_w[l], conv_ssm_b[l].reshape(1, conv_dim),
               pad_h(dt_bias[l]), pad_h(-jnp.exp(a_log[l].astype(F32))),
               jnp.repeat(d_skip[l], HEADDIM).reshape(1, d_inner), ssm_norm_w[l].reshape(1, d_inner),
               w_ssm_out[l].astype(BF16), expand, d_inner, s, ts)

    x1, h2, logits_t = _conf(
        x2d, shift1, scale1, n1w, w_gg, y_a, gate1, shift2, scale2, conf_dw_w[l], conf_dw_b[l].reshape(1, d),
        conf_ln_w[l].reshape(1, d), conf_ln_b[l].reshape(1, d), w_conf_out[l].astype(BF16),
        w_o[l].astype(BF16), norm2_w[l].reshape(1, d), w_router[l].T, b_router[l].reshape(ne, 1), s, ts)

    tt = _pick(s, MOE_TILE)
    n_tiles = t // tt
    w_t, pos_t, tcnt = _route(logits_t, tt)

    rows = MOE_ROWS
    n_blocks = (t * TOPK) // rows + ne
    tcnt = tcnt[..., 0].reshape(n_tiles, ne).astype(I32)
    counts = jnp.sum(tcnt, axis=0)
    padded = ((counts + rows - 1) // rows) * rows
    pad_end = jnp.cumsum(padded)
    pad_start = pad_end - padded
    run_base = (pad_start[None, :] + jnp.cumsum(tcnt, axis=0) - tcnt).reshape(-1)
    run_cnt = tcnt.reshape(-1)
    real_end = pad_start + counts
    blk_start = jnp.arange(n_blocks, dtype=I32) * rows
    block_e = jnp.minimum(jnp.sum((pad_end[None, :] <= blk_start[:, None]).astype(I32), axis=1), ne - 1)
    n_used = (pad_end[ne - 1:ne] // rows).astype(I32)

    x_sorted = _dispatch(run_base, run_cnt, real_end, n_used, h2, pos_t, (n_blocks + 1) * rows, rows, tt)

    wg, wl = _wprep(w_gate_up[l], _pick(d, 512))
    bgu = b_gate_up[l]
    dff = wg.shape[2]
    outs = _experts(block_e, n_used, x_sorted, wg, wl,
                    bgu[:, 0::2].reshape(ne, 1, dff), bgu[:, 1::2].reshape(ne, 1, dff),
                    w_down[l].astype(BF16), b_down[l].reshape(ne, 1, d), rows)

    pos_tm = jnp.transpose(pos_t, (0, 2, 1)).reshape(t, TOPK)
    w_tm = jnp.transpose(w_t, (0, 2, 1)).reshape(t, TOPK)
    out = _combine(run_base, run_cnt, outs, pos_tm, w_tm, x1, gate2, final_norm_w.reshape(1, d), s, tt)
    return out.reshape(bsz, s, d)
```
